```python
import jax, jax.numpy as jnp
from jax import lax
import numpy as np

D_MODEL = 4096
BATCH = 2
SEQ = 4096
DEPTH = 2

GRID_W = 64
CTX_LEN = 256
N_MIXERS = 2
N_FOURIER_LAYERS = (DEPTH + 1) // 2
N_MLA_LAYERS = DEPTH // 2
FOURIER_GROUPS = 4
FOURIER_GROUP_DIM = D_MODEL // FOURIER_GROUPS
MLA_HEADS = 64
Q_LORA = 1536
KV_LORA = 512
QK_NOPE = 128
QK_ROPE = 64
V_HEAD = 128
QK_HEAD = QK_NOPE + QK_ROPE
SOFTMAX_SCALE = QK_HEAD ** -0.5
ROPE_BASE = 10000.0
Q_BLOCK = 128
FFN_HIDDEN = ((8 * D_MODEL + 3 * 256 - 1) // (3 * 256)) * 256
N_MOD = 6
EPS = 1e-6
ADA_INIT_GAIN = 0.5

kernel_name = "fnet_mla_interleaved_dit_block"


def rms_norm(t, g):
    tf = t.astype(jnp.float32)
    y = tf * lax.rsqrt(jnp.mean(tf * tf, axis=-1, keepdims=True) + EPS)
    return y.astype(t.dtype) * g


def modulate(h, shift, scale):
    return h * (1 + scale) + shift


def swiglu(h, w_gate, w_up, w_down):
    return (jax.nn.silu(h @ w_gate) * (h @ w_up)) @ w_down


def axial_rope_tables(n, dtype):
    rows_n = n // GRID_W
    row = jnp.repeat(jnp.arange(rows_n, dtype=jnp.float32), GRID_W)
    col = jnp.tile(jnp.arange(GRID_W, dtype=jnp.float32), rows_n)
    n_freq = QK_ROPE // 4
    inv_freq = ROPE_BASE ** (-jnp.arange(n_freq, dtype=jnp.float32) / n_freq)
    ang = jnp.concatenate([row[:, None] * inv_freq, col[:, None] * inv_freq], axis=-1)
    return jnp.cos(ang).astype(dtype), jnp.sin(ang).astype(dtype)


def apply_axial_rope(t, cos, sin):
    q = QK_ROPE // 4
    bshape = (t.shape[1],) + (1,) * (t.ndim - 3) + (2 * q,)
    cos = cos.reshape(bshape)
    sin = sin.reshape(bshape)
    cr, sr, cc, sc = cos[..., :q], sin[..., :q], cos[..., q:], sin[..., q:]
    r1, r2, c1, c2 = t[..., :q], t[..., q:2 * q], t[..., 2 * q:3 * q], t[..., 3 * q:]
    return jnp.concatenate([r1 * cr - r2 * sr, r2 * cr + r1 * sr,
                            c1 * cc - c2 * sc, c2 * cc + c1 * sc], axis=-1)


def fourier_2d(h):
    b, n, _ = h.shape
    u = h.astype(jnp.float32).reshape(b, n, FOURIER_GROUPS, FOURIER_GROUP_DIM)
    f = jnp.fft.fft2(u, axes=(1, 3), norm="ortho").real
    return f.reshape(b, n, D_MODEL).astype(h.dtype)


def fourier_mix(h, hc, w_out, need_ctx):
    y = fourier_2d(h) @ w_out
    yc = fourier_2d(hc) @ w_out if need_ctx else None
    return y, yc


def mla_queries(c_q, g_q, w_uq, rope):
    b, n, _ = c_q.shape
    q = (rms_norm(c_q, g_q) @ w_uq).reshape(b, n, MLA_HEADS, QK_HEAD)
    q_nope, q_rope = q[..., :QK_NOPE], q[..., QK_NOPE:]
    if rope is not None:
        q_rope = apply_axial_rope(q_rope, *rope)
    return q_nope, q_rope


def mla_keys_values(c_kv, k_rope, g_kv, w_ukv, rope):
    b, n, _ = c_kv.shape
    kv = (rms_norm(c_kv, g_kv) @ w_ukv).reshape(b, n, MLA_HEADS, QK_NOPE + V_HEAD)
    k_nope, v = kv[..., :QK_NOPE], kv[..., QK_NOPE:]
    if rope is not None:
        k_rope = apply_axial_rope(k_rope, *rope)
    return k_nope, k_rope, v


def mla_attend(q_nope, q_rope, k_nope, k_rope, v):
    s = (jnp.einsum('bqhd,bkhd->bhqk', q_nope, k_nope)
         + jnp.einsum('bqhr,bkr->bhqk', q_rope, k_rope))
    p = jax.nn.softmax(s.astype(jnp.float32) * SOFTMAX_SCALE, axis=-1).astype(v.dtype)
    return jnp.einsum('bhqk,bkhd->bqhd', p, v)


def mla_mix(h, hc, w_a, g_q, g_kv, w_uq, w_ukv, w_o, rope, need_ctx):
    b, n, _ = h.shape
    nc = hc.shape[1]
    cq_l, ckv_l, kr_l = jnp.split(h @ w_a, [Q_LORA, Q_LORA + KV_LORA], axis=-1)
    ql_n, ql_r = mla_queries(cq_l, g_q, w_uq, rope)
    kl_n, kl_r, vl = mla_keys_values(ckv_l, kr_l, g_kv, w_ukv, rope)
    if need_ctx:
        cq_c, ckv_c, kr_c = jnp.split(hc @ w_a, [Q_LORA, Q_LORA + KV_LORA], axis=-1)
    else:
        ckv_c, kr_c = jnp.split(hc @ w_a[:, Q_LORA:], [KV_LORA], axis=-1)
    kc_n, kc_r, vc = mla_keys_values(ckv_c, kr_c, g_kv, w_ukv, None)
    k_n = jnp.concatenate([kc_n, kl_n], axis=1)
    k_r = jnp.concatenate([kc_r, kl_r], axis=1)
    v_all = jnp.concatenate([vc, vl], axis=1)
    nb = n // Q_BLOCK

    def to_blocks(t):
        return t.reshape((b, nb, Q_BLOCK) + t.shape[2:]).swapaxes(0, 1)

    o = lax.map(lambda qs: mla_attend(qs[0], qs[1], k_n, k_r, v_all), (to_blocks(ql_n), to_blocks(ql_r)))
    o = o.swapaxes(0, 1).reshape(b, n, MLA_HEADS * V_HEAD)
    y = o @ w_o
    yc = None
    if need_ctx:
        qc_n, qc_r = mla_queries(cq_c, g_q, w_uq, None)
        oc = mla_attend(qc_n, qc_r, kc_n, kc_r, vc).reshape(b, nc, MLA_HEADS * V_HEAD)
        yc = oc @ w_o
    return y, yc


def setup_inputs(seed: int = 0) -> dict:
    key = jax.random.key(seed)
    ks = jax.random.split(key, 20)
    f32 = jnp.float32

    def nrm(k, shape, scale):
        return jax.random.normal(k, shape, f32) * scale

    hid_q = MLA_HEADS * QK_HEAD
    hid_kv = MLA_HEADS * (QK_NOPE + V_HEAD)
    hid_o = MLA_HEADS * V_HEAD
    return {
        "x": nrm(ks[0], (BATCH, SEQ, D_MODEL), 1.0),
        "c": nrm(ks[1], (BATCH, D_MODEL), 1.0),
        "ctx": nrm(ks[2], (BATCH, CTX_LEN, D_MODEL), 1.0),
        "c_ctx": nrm(ks[3], (D_MODEL,), 1.0),
        "w_ada": nrm(ks[4], (DEPTH, D_MODEL, N_MOD * D_MODEL), ADA_INIT_GAIN * D_MODEL ** -0.5),
        "b_ada": nrm(ks[5], (DEPTH, N_MOD * D_MODEL), 0.02),
        "g_mix": 1.0 + nrm(ks[6], (DEPTH, D_MODEL), 0.1),
        "g_ffn": 1.0 + nrm(ks[7], (DEPTH, D_MODEL), 0.1),
        "fourier_w_out": nrm(ks[8], (N_FOURIER_LAYERS, D_MODEL, D_MODEL), D_MODEL ** -0.5),
        "mla_w_a": nrm(ks[9], (N_MLA_LAYERS, D_MODEL, Q_LORA + KV_LORA + QK_ROPE), D_MODEL ** -0.5),
        "mla_g_q": 1.0 + nrm(ks[10], (N_MLA_LAYERS, Q_LORA), 0.1),
        "mla_g_kv": 1.0 + nrm(ks[11], (N_MLA_LAYERS, KV_LORA), 0.1),
        "mla_w_uq": nrm(ks[12], (N_MLA_LAYERS, Q_LORA, hid_q), Q_LORA ** -0.5),
        "mla_w_ukv": nrm(ks[13], (N_MLA_LAYERS, KV_LORA, hid_kv), KV_LORA ** -0.5),
        "mla_w_o": nrm(ks[14], (N_MLA_LAYERS, hid_o, D_MODEL), hid_o ** -0.5),
        "w_gate": nrm(ks[15], (DEPTH, D_MODEL, FFN_HIDDEN), D_MODEL ** -0.5),
        "w_up": nrm(ks[16], (DEPTH, D_MODEL, FFN_HIDDEN), D_MODEL ** -0.5),
        "w_down": nrm(ks[17], (DEPTH, FFN_HIDDEN, D_MODEL), FFN_HIDDEN ** -0.5),
        "g_final": 1.0 + nrm(ks[18], (D_MODEL,), 0.1),
    }


def reference(x, c, ctx, c_ctx, w_ada, b_ada, g_mix, g_ffn, fourier_w_out, mla_w_a, mla_g_q, mla_g_kv,
              mla_w_uq, mla_w_ukv, mla_w_o, w_gate, w_up, w_down, g_final):
    rope = axial_rope_tables(x.shape[1], x.dtype)
    xc = ctx
    for i in range(DEPTH):
        need_ctx = i < DEPTH - 1
        j = i // N_MIXERS
        mod = jax.nn.silu(c) @ w_ada[i] + b_ada[i]
        mod_c = jax.nn.silu(c_ctx) @ w_ada[i] + b_ada[i]
        sh_m, sc_m, gt_m, sh_f, sc_f, gt_f = jnp.split(mod[:, None, :], N_MOD, axis=-1)
        csh_m, csc_m, cgt_m, csh_f, csc_f, cgt_f = jnp.split(mod_c, N_MOD, axis=-1)
        h = modulate(rms_norm(x, g_mix[i]), sh_m, sc_m)
        hc = modulate(rms_norm(xc, g_mix[i]), csh_m, csc_m)
        if i % N_MIXERS == 0:
            y, yc = fourier_mix(h, hc, fourier_w_out[j], need_ctx)
        else:
            y, yc = mla_mix(h, hc, mla_w_a[j], mla_g_q[j], mla_g_kv[j], mla_w_uq[j], mla_w_ukv[j],
                            mla_w_o[j], rope, need_ctx)
        x = x + gt_m * y
        x = x + gt_f * swiglu(modulate(rms_norm(x, g_ffn[i]), sh_f, sc_f), w_gate[i], w_up[i], w_down[i])
        if need_ctx:
            xc = xc + cgt_m * yc
            xc = xc + cgt_f * swiglu(modulate(rms_norm(xc, g_ffn[i]), csh_f, csc_f),
                                     w_gate[i], w_up[i], w_down[i])
    return rms_norm(x, g_final)
```

```python
import functools

import jax
import jax.numpy as jnp
import numpy as np
from jax import lax
from jax.experimental import pallas as pl
from jax.experimental.pallas import tpu as pltpu

_F32 = jnp.float32
_BF16 = jnp.bfloat16

GRID_W = 64
FOURIER_GROUPS = 4
MLA_HEADS = 64
Q_LORA = 1536
KV_LORA = 512
QK_NOPE = 128
QK_ROPE = 64
V_HEAD = 128
ROPE_BASE = 10000.0
EPS = 1e-6
N_MOD = 6

_LANES = 128
_SUBLANES_BF16 = 16
_VMEM_CAP_BYTES = 56 * 1024 * 1024
_HEAD_SLOT = 2 * _LANES


def _tile(dim, pref, align):
    if dim <= pref:
        return dim
    t = (pref // align) * align
    while t >= align:
        if dim % t == 0:
            return t
        t -= align
    return dim


def _nbytes(shape, dtype):
    return int(np.prod(shape)) * jnp.dtype(dtype).itemsize


def _params(semantics, vmem_bytes):
    return pltpu.CompilerParams(dimension_semantics=semantics,
                                vmem_limit_bytes=int(min(_VMEM_CAP_BYTES, vmem_bytes)))


def _ada_body(c_ref, w_ref, b_ref, o_ref):
    c = c_ref[...]
    s = c * jax.nn.sigmoid(c)
    acc = jnp.dot(s.astype(_BF16), w_ref[...].astype(_BF16), preferred_element_type=_F32)
    o_ref[...] = acc + b_ref[...]


def _ada_modulation(cc, w_ada, b_ada):
    depth, d, n = w_ada.shape
    r = cc.shape[0]
    tn = _tile(n, 512, _LANES)
    vmem = 2 * _nbytes((d, tn), _F32) + _nbytes((d, tn), _BF16) + 8 * _nbytes((r, tn), _F32) + (4 << 20)
    return pl.pallas_call(
        _ada_body,
        grid=(depth, n // tn),
        in_specs=[
            pl.BlockSpec((r, d), lambda l, j: (0, 0)),
            pl.BlockSpec((None, d, tn), lambda l, j: (l, 0, j)),
            pl.BlockSpec((None, 1, tn), lambda l, j: (l, 0, j)),
        ],
        out_specs=pl.BlockSpec((None, r, tn), lambda l, j: (l, 0, j)),
        out_shape=jax.ShapeDtypeStruct((depth, r, n), _F32),
        compiler_params=_params(("parallel", "parallel"), vmem),
        name="ada_modulation",
    )(cc, w_ada, b_ada.reshape(depth, 1, n))


def _norm_body(*refs, modulate):
    if modulate:
        x_ref, g_ref, sc_ref, sh_ref, o_ref = refs
    else:
        x_ref, g_ref, o_ref = refs
    x = x_ref[...]
    y = x * lax.rsqrt(jnp.mean(x * x, axis=-1, keepdims=True) + EPS) * g_ref[...]
    if modulate:
        y = y * (1.0 + sc_ref[...]) + sh_ref[...]
    o_ref[...] = y.astype(o_ref.dtype)


def _rms_norm(x, g, scale=None, shift=None, *, out_dtype):
    b, n, d = x.shape
    tr = _tile(n, 256, 8)
    modulate = scale is not None
    in_specs = [pl.BlockSpec((None, tr, d), lambda bb, i: (bb, i, 0)),
                pl.BlockSpec((1, d), lambda bb, i: (0, 0))]
    args = [x, g.reshape(1, d)]
    if modulate:
        per_batch = scale.shape[0] > 1
        mod_spec = pl.BlockSpec((None, 1, d), (lambda bb, i: (bb, 0, 0)) if per_batch else (lambda bb, i: (0, 0, 0)))
        in_specs += [mod_spec, mod_spec]
        args += [scale, shift]
    vmem = 2 * _nbytes((tr, d), _F32) + 2 * _nbytes((tr, d), out_dtype) + 3 * _nbytes((tr, d), _F32) + (2 << 20)
    return pl.pallas_call(
        functools.partial(_norm_body, modulate=modulate),
        grid=(b, n // tr),
        in_specs=in_specs,
        out_specs=pl.BlockSpec((None, tr, d), lambda bb, i: (bb, i, 0)),
        out_shape=jax.ShapeDtypeStruct((b, n, d), out_dtype),
        compiler_params=_params(("parallel", "parallel"), vmem),
        name="rms_norm_modulate" if modulate else "rms_norm",
    )(*args)


def _rope_rotate(t, cos, sin):
    lane = lax.broadcasted_iota(jnp.int32, t.shape, 1)
    even_chunk = (lane // (QK_ROPE // 4)) % 2 == 0
    quarter = QK_ROPE // 4
    swapped = jnp.where(even_chunk, pltpu.roll(t, _LANES - quarter, 1), pltpu.roll(t, quarter, 1))
    return t * cos + swapped * sin


def _mm_body(*refs, n_b, epilogue, nk, q_scale):
    a_ref = refs[0]
    b_refs = refs[1:1 + n_b]
    pos = 1 + n_b
    extra = ()
    if epilogue in ("resgate", "qrope"):
        extra = refs[pos:pos + 2]
        pos += 2
    o_ref = refs[pos]
    acc_refs = refs[pos + 1:]

    def finish(accs):
        if epilogue == "plain":
            out = accs[0]
        elif epilogue == "swiglu":
            g, u = accs
            out = g * jax.nn.sigmoid(g) * u
        elif epilogue == "resgate":
            res_ref, gate_ref = extra
            out = res_ref[...] + gate_ref[...] * accs[0]
        elif epilogue == "qrope":
            cos_ref, sin_ref = extra
            acc = accs[0]
            cos = cos_ref[...]
            sin = sin_ref[...]
            parts = []
            for h in range(acc.shape[1] // _HEAD_SLOT):
                lo = h * _HEAD_SLOT
                parts.append(acc[:, lo:lo + _LANES] * q_scale)
                parts.append(_rope_rotate(acc[:, lo + _LANES:lo + _HEAD_SLOT], cos, sin) * q_scale)
            out = jnp.concatenate(parts, axis=1)
        o_ref[...] = out.astype(o_ref.dtype)

    def dots():
        a = a_ref[...]
        return [jnp.dot(a, b_ref[...], preferred_element_type=_F32) for b_ref in b_refs]

    if nk == 1:
        finish(dots())
        return

    k = pl.program_id(3)

    @pl.when(k == 0)
    def _():
        for acc_ref, d in zip(acc_refs, dots()):
            acc_ref[...] = d

    @pl.when(k > 0)
    def _():
        for acc_ref, d in zip(acc_refs, dots()):
            acc_ref[...] += d

    @pl.when(k == nk - 1)
    def _():
        finish([acc_ref[...] for acc_ref in acc_refs])


def _matmul(a, b, *, tm, tn, tk=None, out_dtype, b2=None, resid=None, gate=None, rope=None,
            q_scale=1.0, name):
    ba, m, kdim = a.shape
    bb_, _, n = b.shape
    bt = max(ba, bb_, resid.shape[0] if resid is not None else 1)
    tk = kdim if tk is None else tk
    nk = kdim // tk
    assert m % tm == 0 and n % tn == 0 and kdim % tk == 0, (name, a.shape, b.shape, tm, tn, tk)

    def bidx(count):
        return (lambda z: z) if count > 1 else (lambda z: 0)

    ai, bi = bidx(ba), bidx(bb_)
    in_specs = [pl.BlockSpec((None, tm, tk), lambda z, i, j, k: (ai(z), i, k)),
                pl.BlockSpec((None, tk, tn), lambda z, i, j, k: (bi(z), k, j))]
    args = [a, b]
    n_b = 1
    epilogue = "plain"
    blocks = [((tm, tk), a.dtype), ((tk, tn), b.dtype)]
    if b2 is not None:
        in_specs.append(pl.BlockSpec((None, tk, tn), lambda z, i, j, k: (bi(z), k, j)))
        args.append(b2)
        blocks.append(((tk, tn), b2.dtype))
        n_b = 2
        epilogue = "swiglu"
    if resid is not None:
        gi = bidx(gate.shape[0])
        in_specs += [pl.BlockSpec((None, tm, tn), lambda z, i, j, k: (z, i, j)),
                     pl.BlockSpec((None, 1, tn), lambda z, i, j, k: (gi(z), 0, j))]
        args += [resid, gate]
        blocks.append(((tm, tn), resid.dtype))
        epilogue = "resgate"
    if rope is not None:
        assert tn % _HEAD_SLOT == 0
        table_spec = pl.BlockSpec((tm, _LANES), lambda z, i, j, k: (i, 0))
        in_specs += [table_spec, table_spec]
        args += list(rope)
        blocks += [((tm, _LANES), _F32)] * 2
        epilogue = "qrope"
    blocks.append(((tm, tn), out_dtype))
    scratch = [pltpu.VMEM((tm, tn), _F32) for _ in range(n_b)] if nk > 1 else []
    vmem = 2 * sum(_nbytes(s, dt) for s, dt in blocks) + (n_b + 2) * _nbytes((tm, tn), _F32) + (2 << 20)
    return pl.pallas_call(
        functools.partial(_mm_body, n_b=n_b, epilogue=epilogue, nk=nk, q_scale=q_scale),
        grid=(bt, m // tm, n // tn, nk),
        in_specs=in_specs,
        out_specs=pl.BlockSpec((None, tm, tn), lambda z, i, j, k: (z, i, j)),
        out_shape=jax.ShapeDtypeStruct((bt, m, n), out_dtype),
        scratch_shapes=scratch,
        compiler_params=_params(("parallel", "parallel", "parallel", "arbitrary"), vmem),
        name=name,
    )(*args)


def _dft_cos_sin(n, scale):
    idx = jnp.arange(n, dtype=jnp.int32)
    phase = (idx[:, None] * idx[None, :]) % n
    ang = phase.astype(_F32) * (2.0 * np.pi / n)
    return (jnp.cos(ang) * scale).astype(_BF16), (jnp.sin(ang) * scale).astype(_BF16)


def _channel_dft(h, cs):
    b, n, d = h.shape
    dg = cs.shape[1]
    groups = d // dg
    tm = _tile(n, 1024, _SUBLANES_BF16)
    vmem = 2 * (_nbytes((tm, dg), _BF16) * 2 + _nbytes((dg, dg), _BF16)) + 2 * _nbytes((tm, dg), _F32) + (2 << 20)
    return pl.pallas_call(
        functools.partial(_mm_body, n_b=1, epilogue="plain", nk=1, q_scale=1.0),
        grid=(b, n // tm, groups, 2),
        in_specs=[pl.BlockSpec((None, tm, dg), lambda z, i, g, c: (z, i, g)),
                  pl.BlockSpec((None, dg, dg), lambda z, i, g, c: (c, 0, 0))],
        out_specs=pl.BlockSpec((None, None, tm, dg), lambda z, i, g, c: (z, c, i, g)),
        out_shape=jax.ShapeDtypeStruct((b, 2, n, d), _BF16),
        compiler_params=_params(("parallel",) * 4, vmem),
        name="fourier_channel_dft",
    )(h, cs)


def _fourier_2d(h):
    b, n, d = h.shape
    dg = d // FOURIER_GROUPS
    cc, sc = _dft_cos_sin(dg, dg ** -0.5)
    cn, sn = _dft_cos_sin(n, n ** -0.5)
    z = _channel_dft(h, jnp.stack([cc, sc]))
    a = jnp.concatenate([cn, -sn], axis=1)[None]
    return _matmul(a, z.reshape(b, 2 * n, d), tm=_tile(n, 1024, _SUBLANES_BF16), tn=_tile(d, 1024, _LANES),
                   tk=_tile(2 * n, 2048, _LANES), out_dtype=_BF16, name="fourier_position_dft")


def _mla_prep_body(*refs, with_q, with_rope):
    refs = list(refs)
    a_ref = refs.pop(0)
    gq_ref = refs.pop(0) if with_q else None
    gkv_ref = refs.pop(0)
    cos_ref, sin_ref = (refs.pop(0), refs.pop(0)) if with_rope else (None, None)
    cq_ref = refs.pop(0) if with_q else None
    ckv_ref, kr_ref = refs

    def norm(t, g):
        return t * lax.rsqrt(jnp.mean(t * t, axis=-1, keepdims=True) + EPS) * g

    off = 0
    if with_q:
        cq_ref[...] = norm(a_ref[:, :Q_LORA], gq_ref[...]).astype(cq_ref.dtype)
        off = Q_LORA
    ckv_ref[...] = norm(a_ref[:, off:off + KV_LORA], gkv_ref[...]).astype(ckv_ref.dtype)
    kr = a_ref[:, off + KV_LORA:off + KV_LORA + _LANES]
    if with_rope:
        kr = _rope_rotate(kr, cos_ref[...], sin_ref[...])
    kr_ref[...] = kr.astype(kr_ref.dtype)


def _mla_prep(a, g_q, g_kv, rope):
    b, n, w = a.shape
    with_q = g_q is not None
    with_rope = rope is not None
    tr = _tile(n, 256, _SUBLANES_BF16)
    row = lambda z, i: (z, i, 0)
    in_specs = [pl.BlockSpec((None, tr, w), row)]
    args = [a]
    if with_q:
        in_specs.append(pl.BlockSpec((1, Q_LORA), lambda z, i: (0, 0)))
        args.append(g_q.reshape(1, Q_LORA))
    in_specs.append(pl.BlockSpec((1, KV_LORA), lambda z, i: (0, 0)))
    args.append(g_kv.reshape(1, KV_LORA))
    if with_rope:
        in_specs += [pl.BlockSpec((tr, _LANES), lambda z, i: (i, 0))] * 2
        args += list(rope)
    out_specs, out_shape = [], []
    if with_q:
        out_specs.append(pl.BlockSpec((None, tr, Q_LORA), row))
        out_shape.append(jax.ShapeDtypeStruct((b, n, Q_LORA), _BF16))
    out_specs += [pl.BlockSpec((None, tr, KV_LORA), row), pl.BlockSpec((None, tr, _LANES), row)]
    out_shape += [jax.ShapeDtypeStruct((b, n, KV_LORA), _BF16), jax.ShapeDtypeStruct((b, n, _LANES), _BF16)]
    vmem = 6 * _nbytes((tr, w), _F32) + (2 << 20)
    return pl.pallas_call(
        functools.partial(_mla_prep_body, with_q=with_q, with_rope=with_rope),
        grid=(b, n // tr),
        in_specs=in_specs,
        out_specs=out_specs,
        out_shape=out_shape,
        compiler_params=_params(("parallel", "parallel"), vmem),
        name="mla_prep_latent" if with_q else "mla_prep_context",
    )(*args)


def _attn_body(q_ref, kv_ref, kr_ref, o_ref):
    q = q_ref[...]
    kv = kv_ref[...]
    k = jnp.concatenate([kv[:, :QK_NOPE], kr_ref[...]], axis=1)
    s = lax.dot_general(q, k, (((1,), (1,)), ((), ())), preferred_element_type=_F32)
    p = jnp.exp(s - jnp.max(s, axis=1, keepdims=True))
    l = jnp.sum(p, axis=1, keepdims=True)
    o = jnp.dot(p.astype(_BF16), kv[:, QK_NOPE:], preferred_element_type=_F32)
    o_ref[...] = (o / l).astype(o_ref.dtype)


def _attention(q, kv, kr):
    b, n, _ = q.shape
    nk = kv.shape[1]
    tq = _tile(n, 256, _SUBLANES_BF16)
    vmem = (4 * _nbytes((tq, _HEAD_SLOT), _BF16) + 3 * _nbytes((nk, _HEAD_SLOT), _BF16)
            + 2 * _nbytes((nk, _LANES), _BF16) + 4 * _nbytes((tq, nk), _F32) + (4 << 20))
    return pl.pallas_call(
        _attn_body,
        grid=(b, MLA_HEADS, n // tq),
        in_specs=[pl.BlockSpec((None, tq, _HEAD_SLOT), lambda z, h, i: (z, i, h)),
                  pl.BlockSpec((None, nk, _HEAD_SLOT), lambda z, h, i: (z, 0, h)),
                  pl.BlockSpec((None, nk, _LANES), lambda z, h, i: (z, 0, 0))],
        out_specs=pl.BlockSpec((None, tq, V_HEAD), lambda z, h, i: (z, i, h)),
        out_shape=jax.ShapeDtypeStruct((b, n, MLA_HEADS * V_HEAD), _BF16),
        compiler_params=_params(("parallel", "parallel", "parallel"), vmem),
        name="mla_attention",
    )(q, kv, kr)


def _rope_tables(n):
    rows_n = n // GRID_W
    row = jnp.repeat(jnp.arange(rows_n, dtype=_F32), GRID_W)
    col = jnp.tile(jnp.arange(GRID_W, dtype=_F32), rows_n)
    n_freq = QK_ROPE // 4
    inv_freq = ROPE_BASE ** (-jnp.arange(n_freq, dtype=_F32) / n_freq)
    ar, ac = row[:, None] * inv_freq, col[:, None] * inv_freq
    zeros = jnp.zeros((n, _LANES - QK_ROPE), _F32)
    cos = jnp.concatenate([jnp.cos(ar), jnp.cos(ar), jnp.cos(ac), jnp.cos(ac), zeros], axis=1)
    sin = jnp.concatenate([-jnp.sin(ar), jnp.sin(ar), -jnp.sin(ac), jnp.sin(ac), zeros], axis=1)
    return cos, sin


def _mla_mix(h, hc, w_a, g_q, g_kv, w_uq, w_ukv, w_o, x, gate):
    b, n, d = h.shape
    nc = hc.shape[1]
    rope = _rope_tables(n)
    a_width = Q_LORA + KV_LORA + QK_ROPE
    a_pad = -(-a_width // _HEAD_SLOT) * _HEAD_SLOT
    w_a_pad = jnp.pad(w_a, ((0, 0), (0, a_pad - a_width))).astype(_BF16)[None]
    tm = _tile(n, 1024, _SUBLANES_BF16)
    a_l = _matmul(h, w_a_pad, tm=tm, tn=_HEAD_SLOT, out_dtype=_F32, name="mla_down_latent")
    a_c = _matmul(hc, w_a_pad[:, :, Q_LORA:], tm=_tile(nc, 1024, _SUBLANES_BF16), tn=_HEAD_SLOT,
                  out_dtype=_F32, name="mla_down_context")
    cq, ckv_l, kr_l = _mla_prep(a_l, g_q, g_kv, rope)
    ckv_c, kr_c = _mla_prep(a_c, None, g_kv, None)
    ckv = jnp.concatenate([ckv_c, ckv_l], axis=1)
    kr = jnp.concatenate([kr_c, kr_l], axis=1)

    w_uq_h = w_uq.reshape(Q_LORA, MLA_HEADS, QK_NOPE + QK_ROPE)
    w_uq_pad = jnp.pad(w_uq_h, ((0, 0), (0, 0), (0, _HEAD_SLOT - QK_NOPE - QK_ROPE)))
    w_uq_pad = w_uq_pad.reshape(Q_LORA, MLA_HEADS * _HEAD_SLOT).astype(_BF16)[None]
    q = _matmul(cq, w_uq_pad, tm=tm, tn=_tile(MLA_HEADS * _HEAD_SLOT, 1024, _HEAD_SLOT), out_dtype=_BF16,
                rope=rope, q_scale=float((QK_NOPE + QK_ROPE) ** -0.5), name="mla_q_up")
    nkv = nc + n
    kv = _matmul(ckv, w_ukv.astype(_BF16)[None], tm=_tile(nkv, 1088, _SUBLANES_BF16),
                 tn=_tile(MLA_HEADS * _HEAD_SLOT, 2048, _HEAD_SLOT), out_dtype=_BF16, name="mla_kv_up")
    o = _attention(q, kv, kr)
    hid_o = MLA_HEADS * V_HEAD
    return _matmul(o, w_o.astype(_BF16)[None], tm=tm, tn=_tile(d, 512, _LANES), tk=_tile(hid_o, 4096, _LANES),
                   out_dtype=_F32, resid=x, gate=gate, name="mla_out_proj")


def _ffn(x, h, w_gate, w_up, w_down, gate):
    b, n, d = x.shape
    hidden = w_gate.shape[1]
    tm = _tile(n, 1024, _SUBLANES_BF16)
    u = _matmul(h, w_gate[None], b2=w_up[None], tm=tm, tn=_tile(hidden, 256, _LANES), out_dtype=_BF16,
                name="ffn_gate_up")
    half = hidden // 2
    tk = half if half % _LANES == 0 else hidden
    return _matmul(u, w_down[None], tm=tm, tn=_tile(d, 512, _LANES), tk=tk, out_dtype=_F32, resid=x, gate=gate,
                   name="ffn_down")


def kernel(x, c, ctx, c_ctx, w_ada, b_ada, g_mix, g_ffn, fourier_w_out, mla_w_a, mla_g_q, mla_g_kv, mla_w_uq,
           mla_w_ukv, mla_w_o, w_gate, w_up, w_down, g_final):
    b, n, d = x.shape
    depth = w_ada.shape[0]
    cond = jnp.concatenate([c, c_ctx[None]], axis=0)
    cond = jnp.pad(cond, ((0, -(b + 1) % 8), (0, 0)))
    mod = _ada_modulation(cond, w_ada, b_ada)

    xc = ctx
    for i in range(depth):
        need_ctx = i < depth - 1
        j = i // 2
        lat = [mod[i, :b, s * d:(s + 1) * d].reshape(b, 1, d) for s in range(N_MOD)]
        con = [mod[i, b:b + 1, s * d:(s + 1) * d].reshape(1, 1, d) for s in range(N_MOD)]
        h = _rms_norm(x, g_mix[i], lat[1], lat[0], out_dtype=_BF16)
        hc = _rms_norm(xc, g_mix[i], con[1], con[0], out_dtype=_BF16)
        tm = _tile(n, 1024, _SUBLANES_BF16)
        if i % 2 == 0:
            w_out = fourier_w_out[j].astype(_BF16)[None]
            x = _matmul(_fourier_2d(h), w_out, tm=tm, tn=_tile(d, 512, _LANES), out_dtype=_F32,
                        resid=x, gate=lat[2], name="fourier_out_proj")
            if need_ctx:
                xc = _matmul(_fourier_2d(hc), w_out, tm=_tile(xc.shape[1], 1024, _SUBLANES_BF16),
                             tn=_tile(d, 512, _LANES), out_dtype=_F32, resid=xc, gate=con[2],
                             name="fourier_out_proj_ctx")
        else:
            assert not need_ctx, "context queries are only implemented for the Fourier mixer layers"
            x = _mla_mix(h, hc, mla_w_a[j], mla_g_q[j], mla_g_kv[j], mla_w_uq[j], mla_w_ukv[j], mla_w_o[j],
                         x, lat[2])
        wg, wu, wd = w_gate[i].astype(_BF16), w_up[i].astype(_BF16), w_down[i].astype(_BF16)
        x = _ffn(x, _rms_norm(x, g_ffn[i], lat[4], lat[3], out_dtype=_BF16), wg, wu, wd, lat[5])
        if need_ctx:
            xc = _ffn(xc, _rms_norm(xc, g_ffn[i], con[4], con[3], out_dtype=_BF16), wg, wu, wd, con[5])
    return _rms_norm(x, g_final, out_dtype=x.dtype)
```

```python
import functools
import math

import jax
import jax.numpy as jnp
import numpy as np
from jax import lax
from jax.experimental import pallas as pl
from jax.experimental.pallas import tpu as pltpu

_F32 = jnp.float32
_BF16 = jnp.bfloat16

GRID_W = 64
FOURIER_GROUPS = 4
MLA_HEADS = 64
Q_LORA = 1536
KV_LORA = 512
QK_NOPE = 128
QK_ROPE = 64
V_HEAD = 128
ROPE_BASE = 10000.0
EPS = 1e-6
N_MOD = 6

_LANES = 128
_SUBLANES_BF16 = 16
_VMEM_CAP_BYTES = 56 * 1024 * 1024
_HEAD_SLOT = 2 * _LANES
_ATTN_KEY_CHUNK = 512
_ATTN_LOOKAHEAD = 4
_LOG2E = math.log2(math.e)


def _tile(dim, pref, align):
    if dim <= pref:
        return dim
    t = (pref // align) * align
    while t >= align:
        if dim % t == 0:
            return t
        t -= align
    return dim


def _nbytes(shape, dtype):
    return int(np.prod(shape)) * jnp.dtype(dtype).itemsize


def _params(semantics, vmem_bytes):
    return pltpu.CompilerParams(dimension_semantics=semantics,
                                vmem_limit_bytes=int(min(_VMEM_CAP_BYTES, vmem_bytes)))


def _ada_body(c_ref, w_ref, b_ref, o_ref):
    c = c_ref[...]
    s = c * jax.nn.sigmoid(c)
    acc = jnp.dot(s.astype(_BF16), w_ref[...].astype(_BF16), preferred_element_type=_F32)
    o_ref[...] = acc + b_ref[...]


def _ada_modulation(cc, w_ada, b_ada):
    depth, d, n = w_ada.shape
    r = cc.shape[0]
    tn = _tile(n, 512, _LANES)
    vmem = 2 * _nbytes((d, tn), _F32) + _nbytes((d, tn), _BF16) + 8 * _nbytes((r, tn), _F32) + (4 << 20)
    return pl.pallas_call(
        _ada_body,
        grid=(depth, n // tn),
        in_specs=[
            pl.BlockSpec((r, d), lambda l, j: (0, 0)),
            pl.BlockSpec((None, d, tn), lambda l, j: (l, 0, j)),
            pl.BlockSpec((None, 1, tn), lambda l, j: (l, 0, j)),
        ],
        out_specs=pl.BlockSpec((None, r, tn), lambda l, j: (l, 0, j)),
        out_shape=jax.ShapeDtypeStruct((depth, r, n), _F32),
        compiler_params=_params(("parallel", "parallel"), vmem),
        name="ada_modulation",
    )(cc, w_ada, b_ada.reshape(depth, 1, n))


def _norm_body(*refs, modulate):
    if modulate:
        x_ref, g_ref, sc_ref, sh_ref, o_ref = refs
    else:
        x_ref, g_ref, o_ref = refs
    x = x_ref[...]
    y = x * lax.rsqrt(jnp.mean(x * x, axis=-1, keepdims=True) + EPS) * g_ref[...]
    if modulate:
        y = y * (1.0 + sc_ref[...]) + sh_ref[...]
    o_ref[...] = y.astype(o_ref.dtype)


def _rms_norm(x, g, scale=None, shift=None, *, out_dtype):
    b, n, d = x.shape
    tr = _tile(n, 256, 8)
    modulate = scale is not None
    in_specs = [pl.BlockSpec((None, tr, d), lambda bb, i: (bb, i, 0)),
                pl.BlockSpec((1, d), lambda bb, i: (0, 0))]
    args = [x, g.reshape(1, d)]
    if modulate:
        per_batch = scale.shape[0] > 1
        mod_spec = pl.BlockSpec((None, 1, d), (lambda bb, i: (bb, 0, 0)) if per_batch else (lambda bb, i: (0, 0, 0)))
        in_specs += [mod_spec, mod_spec]
        args += [scale, shift]
    vmem = 2 * _nbytes((tr, d), _F32) + 2 * _nbytes((tr, d), out_dtype) + 3 * _nbytes((tr, d), _F32) + (2 << 20)
    return pl.pallas_call(
        functools.partial(_norm_body, modulate=modulate),
        grid=(b, n // tr),
        in_specs=in_specs,
        out_specs=pl.BlockSpec((None, tr, d), lambda bb, i: (bb, i, 0)),
        out_shape=jax.ShapeDtypeStruct((b, n, d), out_dtype),
        compiler_params=_params(("parallel", "parallel"), vmem),
        name="rms_norm_modulate" if modulate else "rms_norm",
    )(*args)


def _rope_rotate(t, cos, sin):
    lane = lax.broadcasted_iota(jnp.int32, t.shape, 1)
    quarter = QK_ROPE // 4
    even_chunk = (lane // quarter) % 2 == 0
    swapped = jnp.where(even_chunk, pltpu.roll(t, _LANES - quarter, 1), pltpu.roll(t, quarter, 1))
    return t * cos + swapped * sin


def _mm_body(*refs, n_b, epilogue, nk, q_scale, nt):
    a_ref = refs[0]
    b_refs = refs[1:1 + n_b]
    pos = 1 + n_b
    extra = ()
    if epilogue in ("resgate", "qrope"):
        extra = refs[pos:pos + 2]
        pos += 2
    o_ref = refs[pos]
    acc_refs = refs[pos + 1:]

    def finish(accs):
        if epilogue == "plain":
            out = accs[0]
        elif epilogue == "swiglu":
            g, u = accs
            out = g * jax.nn.sigmoid(g) * u
        elif epilogue == "resgate":
            res_ref, gate_ref = extra
            out = res_ref[...] + gate_ref[...] * accs[0]
        elif epilogue == "qrope":
            cos_ref, sin_ref = extra
            acc = accs[0]
            cos = cos_ref[...]
            sin = sin_ref[...]
            parts = []
            for h in range(acc.shape[1] // _HEAD_SLOT):
                lo = h * _HEAD_SLOT
                parts.append(acc[:, lo:lo + _LANES] * q_scale)
                parts.append(_rope_rotate(acc[:, lo + _LANES:lo + _HEAD_SLOT], cos, sin) * q_scale)
            out = jnp.concatenate(parts, axis=1)
        o_ref[...] = out.astype(o_ref.dtype)

    def dots():
        a = a_ref[...]
        outs = []
        for b_ref in b_refs:
            b = b_ref[...].astype(_BF16)
            if nt:
                outs.append(lax.dot_general(a, b, (((1,), (1,)), ((), ())), preferred_element_type=_F32))
            else:
                outs.append(jnp.dot(a, b, preferred_element_type=_F32))
        return outs

    if nk == 1:
        finish(dots())
        return

    k = pl.program_id(3)

    @pl.when(k == 0)
    def _():
        for acc_ref, d in zip(acc_refs, dots()):
            acc_ref[...] = d

    @pl.when(k > 0)
    def _():
        for acc_ref, d in zip(acc_refs, dots()):
            acc_ref[...] += d

    @pl.when(k == nk - 1)
    def _():
        finish([acc_ref[...] for acc_ref in acc_refs])


def _matmul(a, b, *, tm, tn, tk=None, out_dtype, b2=None, b_sel=None, nt=False, resid=None, gate=None,
            rope=None, q_scale=1.0, name):
    ba, m, kdim = a.shape
    lb = b.shape[0]
    n = b.shape[1] if nt else b.shape[2]
    b_batched = b_sel is None and lb > 1
    bt = max(ba, lb if b_batched else 1, resid.shape[0] if resid is not None else 1)
    tk = kdim if tk is None else tk
    nk = kdim // tk
    assert m % tm == 0 and n % tn == 0 and kdim % tk == 0, (name, a.shape, b.shape, tm, tn, tk)

    def bidx(count):
        return (lambda z: z) if count > 1 else (lambda z: 0)

    ai = bidx(ba)
    if b_sel is not None:
        bi = lambda z: b_sel
    else:
        bi = bidx(lb)
    if nt:
        b_spec = pl.BlockSpec((None, tn, tk), lambda z, i, j, k: (bi(z), j, k))
        b_block = (tn, tk)
    else:
        b_spec = pl.BlockSpec((None, tk, tn), lambda z, i, j, k: (bi(z), k, j))
        b_block = (tk, tn)
    in_specs = [pl.BlockSpec((None, tm, tk), lambda z, i, j, k: (ai(z), i, k)), b_spec]
    args = [a, b]
    n_b = 1
    epilogue = "plain"
    blocks = [((tm, tk), a.dtype), (b_block, b.dtype)]
    if b2 is not None:
        in_specs.append(b_spec)
        args.append(b2)
        blocks.append((b_block, b2.dtype))
        n_b = 2
        epilogue = "swiglu"
    if resid is not None:
        gi = bidx(gate.shape[0])
        in_specs += [pl.BlockSpec((None, tm, tn), lambda z, i, j, k: (z, i, j)),
                     pl.BlockSpec((None, 1, tn), lambda z, i, j, k: (gi(z), 0, j))]
        args += [resid, gate]
        blocks.append(((tm, tn), resid.dtype))
        epilogue = "resgate"
    if rope is not None:
        assert tn % _HEAD_SLOT == 0
        table_spec = pl.BlockSpec((tm, _LANES), lambda z, i, j, k: (i, 0))
        in_specs += [table_spec, table_spec]
        args += list(rope)
        blocks += [((tm, _LANES), _F32)] * 2
        epilogue = "qrope"
    blocks.append(((tm, tn), out_dtype))
    scratch = [pltpu.VMEM((tm, tn), _F32) for _ in range(n_b)] if nk > 1 else []
    cast_tmp = n_b * _nbytes(b_block, _BF16) if b.dtype != _BF16 else 0
    vmem = (2 * sum(_nbytes(s, dt) for s, dt in blocks) + cast_tmp + (n_b + 2) * _nbytes((tm, tn), _F32)
            + (2 << 20))
    return pl.pallas_call(
        functools.partial(_mm_body, n_b=n_b, epilogue=epilogue, nk=nk, q_scale=q_scale, nt=nt),
        grid=(bt, m // tm, n // tn, nk),
        in_specs=in_specs,
        out_specs=pl.BlockSpec((None, tm, tn), lambda z, i, j, k: (z, i, j)),
        out_shape=jax.ShapeDtypeStruct((bt, m, n), out_dtype),
        scratch_shapes=scratch,
        compiler_params=_params(("parallel", "parallel", "parallel", "arbitrary"), vmem),
        name=name,
    )(*args)


def _dft_cos_sin(n, scale):
    idx = jnp.arange(n, dtype=jnp.int32)
    phase = (idx[:, None] * idx[None, :]) % n
    ang = phase.astype(_F32) * (2.0 * np.pi / n)
    return (jnp.cos(ang) * scale).astype(_BF16), (jnp.sin(ang) * scale).astype(_BF16)


def _fft_stage_tables(n1, n2):
    n = n1 * n2
    c = np.arange(n1)[None, :, None]
    a = np.arange(n1)[None, None, :]
    b = np.arange(n2)[:, None, None]
    theta = 2.0 * np.pi * ((c * (n2 * a + b)) % n) / n
    gr, gi = np.cos(theta) / np.sqrt(n1), -np.sin(theta) / np.sqrt(n1)
    stage_a = np.concatenate([np.concatenate([gr, gi], axis=2), np.concatenate([gi, -gr], axis=2)], axis=1)
    d = np.arange(n2)[:, None]
    bb = np.arange(n2)[None, :]
    phi = 2.0 * np.pi * ((d * bb) % n2) / n2
    stage_b = np.concatenate([np.cos(phi), np.sin(phi)], axis=1) / np.sqrt(n2)
    return jnp.asarray(stage_a, _BF16), jnp.asarray(stage_b, _BF16)


def _channel_dft(h, cs):
    b, n, d = h.shape
    dg = cs.shape[1]
    groups = d // dg
    tm = _tile(n, 1024, _SUBLANES_BF16)
    vmem = 2 * (_nbytes((tm, dg), _BF16) * 2 + _nbytes((dg, dg), _BF16)) + 2 * _nbytes((tm, dg), _F32) + (2 << 20)
    return pl.pallas_call(
        functools.partial(_mm_body, n_b=1, epilogue="plain", nk=1, q_scale=1.0, nt=False),
        grid=(b, n // tm, groups, 2),
        in_specs=[pl.BlockSpec((None, tm, dg), lambda z, i, g, c: (z, i, g)),
                  pl.BlockSpec((None, dg, dg), lambda z, i, g, c: (c, 0, 0))],
        out_specs=pl.BlockSpec((None, None, tm, dg), lambda z, i, g, c: (z, c, i, g)),
        out_shape=jax.ShapeDtypeStruct((b, 2, n, d), _BF16),
        compiler_params=_params(("parallel",) * 4, vmem),
        name="fourier_channel_dft",
    )(h, cs)


def _fft_a_body(z_ref, m_ref, o_ref, *, nb, d, n1):
    for j in range(nb):
        cols = slice(j * d, (j + 1) * d)
        zcat = jnp.concatenate([z_ref[0, :, cols], z_ref[1, :, cols]], axis=0)
        v = jnp.dot(m_ref[j], zcat, preferred_element_type=_F32)
        o_ref[0, j] = v[:n1].astype(o_ref.dtype)
        o_ref[1, j] = v[n1:].astype(o_ref.dtype)


def _fft_b_body(v_ref, m_ref, o_ref):
    vcat = jnp.concatenate([v_ref[0], v_ref[1]], axis=0)
    o_ref[...] = jnp.dot(m_ref[...], vcat, preferred_element_type=_F32).astype(o_ref.dtype)


def _position_fft_real(z):
    b, _, n, d = z.shape
    n1 = math.isqrt(n)
    n2 = n // n1
    assert n1 * n2 == n and n1 % _SUBLANES_BF16 == 0 and n2 % _SUBLANES_BF16 == 0
    stage_a, stage_b = _fft_stage_tables(n1, n2)
    nb = _tile(n2, 4, 1)
    blk = _nbytes((2, n1, nb * d), _BF16)
    v = pl.pallas_call(
        functools.partial(_fft_a_body, nb=nb, d=d, n1=n1),
        grid=(b, n2 // nb),
        in_specs=[pl.BlockSpec((None, 2, n1, nb * d), lambda z_, j: (z_, 0, 0, j)),
                  pl.BlockSpec((nb, 2 * n1, 2 * n1), lambda z_, j: (j, 0, 0))],
        out_specs=pl.BlockSpec((None, 2, nb, n1, d), lambda z_, j: (z_, 0, j, 0, 0)),
        out_shape=jax.ShapeDtypeStruct((b, 2, n2, n1, d), _BF16),
        compiler_params=_params(("parallel", "parallel"), 6 * blk + 4 * _nbytes((2 * n1, d), _F32) + (2 << 20)),
        name="fourier_fft_stage_a",
    )(z.reshape(b, 2, n1, n2 * d), stage_a)
    nc = _tile(n1, 4, 1)
    blk = _nbytes((2, n2, nc * d), _BF16)
    f = pl.pallas_call(
        _fft_b_body,
        grid=(b, n1 // nc),
        in_specs=[pl.BlockSpec((None, 2, n2, nc * d), lambda z_, j: (z_, 0, 0, j)),
                  pl.BlockSpec((n2, 2 * n2), lambda z_, j: (0, 0))],
        out_specs=pl.BlockSpec((None, n2, nc * d), lambda z_, j: (z_, 0, j)),
        out_shape=jax.ShapeDtypeStruct((b, n2, n1 * d), _BF16),
        compiler_params=_params(("parallel", "parallel"), 5 * blk + 2 * _nbytes((n2, nc * d), _F32) + (2 << 20)),
        name="fourier_fft_stage_b",
    )(v.reshape(b, 2, n2, n1 * d), stage_b)
    return f.reshape(b, n, d)


def _fourier_2d(h, cs):
    return _position_fft_real(_channel_dft(h, cs))


def _mla_prep_body(*refs, with_q, with_rope):
    refs = list(refs)
    a_ref = refs.pop(0)
    gq_ref = refs.pop(0) if with_q else None
    gkv_ref = refs.pop(0)
    cos_ref, sin_ref = (refs.pop(0), refs.pop(0)) if with_rope else (None, None)
    cq_ref = refs.pop(0) if with_q else None
    ckv_ref, kr_ref = refs

    def norm(t, g):
        return t * lax.rsqrt(jnp.mean(t * t, axis=-1, keepdims=True) + EPS) * g

    off = 0
    if with_q:
        cq_ref[...] = norm(a_ref[:, :Q_LORA], gq_ref[...]).astype(cq_ref.dtype)
        off = Q_LORA
    ckv_ref[...] = norm(a_ref[:, off:off + KV_LORA], gkv_ref[...]).astype(ckv_ref.dtype)
    kr = a_ref[:, off + KV_LORA:off + KV_LORA + _LANES]
    if with_rope:
        kr = _rope_rotate(kr, cos_ref[...], sin_ref[...])
    kr_ref[...] = kr.astype(kr_ref.dtype)


def _mla_prep(a, g_q, g_kv, rope):
    b, n, w = a.shape
    with_q = g_q is not None
    with_rope = rope is not None
    tr = _tile(n, 256, _SUBLANES_BF16)
    row = lambda z, i: (z, i, 0)
    in_specs = [pl.BlockSpec((None, tr, w), row)]
    args = [a]
    if with_q:
        in_specs.append(pl.BlockSpec((1, Q_LORA), lambda z, i: (0, 0)))
        args.append(g_q.reshape(1, Q_LORA))
    in_specs.append(pl.BlockSpec((1, KV_LORA), lambda z, i: (0, 0)))
    args.append(g_kv.reshape(1, KV_LORA))
    if with_rope:
        in_specs += [pl.BlockSpec((tr, _LANES), lambda z, i: (i, 0))] * 2
        args += list(rope)
    out_specs, out_shape = [], []
    if with_q:
        out_specs.append(pl.BlockSpec((None, tr, Q_LORA), row))
        out_shape.append(jax.ShapeDtypeStruct((b, n, Q_LORA), _BF16))
    out_specs += [pl.BlockSpec((None, tr, KV_LORA), row), pl.BlockSpec((None, tr, _LANES), row)]
    out_shape += [jax.ShapeDtypeStruct((b, n, KV_LORA), _BF16), jax.ShapeDtypeStruct((b, n, _LANES), _BF16)]
    vmem = 6 * _nbytes((tr, w), _F32) + (2 << 20)
    return pl.pallas_call(
        functools.partial(_mla_prep_body, with_q=with_q, with_rope=with_rope),
        grid=(b, n // tr),
        in_specs=in_specs,
        out_specs=out_specs,
        out_shape=out_shape,
        compiler_params=_params(("parallel", "parallel"), vmem),
        name="mla_prep_latent" if with_q else "mla_prep_context",
    )(*args)


def _attn_body(q_ref, k_ref, kr_ref, vt_ref, o_ref, kcat_ref, *, chunks):
    @pl.when(pl.program_id(2) == 0)
    def _():
        kcat_ref[:, :QK_NOPE] = k_ref[...]
        kcat_ref[:, QK_NOPE:] = kr_ref[...]

    q = q_ref[...]

    def scores(start, size):
        return lax.dot_general(kcat_ref[start:start + size, :], q, (((1,), (1,)), ((), ())),
                               preferred_element_type=_F32)

    starts = [sum(chunks[:c]) for c in range(len(chunks))]
    m = l = acc = None
    ahead = [scores(starts[c], chunks[c]) for c in range(min(_ATTN_LOOKAHEAD, len(chunks)))]
    for c, (start, size) in enumerate(zip(starts, chunks)):
        st = ahead.pop(0)
        nxt = c + _ATTN_LOOKAHEAD
        if nxt < len(chunks):
            ahead.append(scores(starts[nxt], chunks[nxt]))
        mc = jnp.max(st, axis=0, keepdims=True)
        m_new = mc if m is None else jnp.maximum(m, mc)
        p = jnp.exp2(st - m_new)
        lc = jnp.sum(p, axis=0, keepdims=True)
        pv = jnp.dot(vt_ref[:, start:start + size], p.astype(_BF16), preferred_element_type=_F32)
        if m is None:
            l, acc = lc, pv
        else:
            alpha = jnp.exp2(m - m_new)
            l = alpha * l + lc
            acc = alpha * acc + pv
        m = m_new
    o_ref[...] = (acc / l).T.astype(o_ref.dtype)


def _attention(q, k, kr, vt):
    b, n, _ = q.shape
    nk = k.shape[1]
    tq = _tile(n, 1024, _LANES)
    chunk = _ATTN_KEY_CHUNK
    chunks = [chunk] * (nk // chunk) + ([nk % chunk] if nk % chunk else [])
    vmem = (4 * _nbytes((tq, _HEAD_SLOT), _BF16) + 5 * _nbytes((nk, _LANES), _BF16)
            + 2 * _nbytes((V_HEAD, nk), _BF16) + 6 * _nbytes((min(chunk, nk), tq), _F32) + (4 << 20))
    return pl.pallas_call(
        functools.partial(_attn_body, chunks=chunks),
        grid=(b, MLA_HEADS, n // tq),
        in_specs=[pl.BlockSpec((None, tq, _HEAD_SLOT), lambda z, h, i: (z, i, h)),
                  pl.BlockSpec((None, nk, QK_NOPE), lambda z, h, i: (z, 0, h)),
                  pl.BlockSpec((None, nk, _LANES), lambda z, h, i: (z, 0, 0)),
                  pl.BlockSpec((None, V_HEAD, nk), lambda z, h, i: (z, h, 0))],
        out_specs=pl.BlockSpec((None, tq, V_HEAD), lambda z, h, i: (z, i, h)),
        out_shape=jax.ShapeDtypeStruct((b, n, MLA_HEADS * V_HEAD), _BF16),
        scratch_shapes=[pltpu.VMEM((nk, _HEAD_SLOT), _BF16)],
        compiler_params=_params(("parallel", "parallel", "arbitrary"), vmem),
        name="mla_attention",
    )(q, k, kr, vt)


def _rope_tables(n):
    rows_n = n // GRID_W
    row = jnp.repeat(jnp.arange(rows_n, dtype=_F32), GRID_W)
    col = jnp.tile(jnp.arange(GRID_W, dtype=_F32), rows_n)
    n_freq = QK_ROPE // 4
    inv_freq = ROPE_BASE ** (-jnp.arange(n_freq, dtype=_F32) / n_freq)
    ar, ac = row[:, None] * inv_freq, col[:, None] * inv_freq
    zeros = jnp.zeros((n, _LANES - QK_ROPE), _F32)
    cos = jnp.concatenate([jnp.cos(ar), jnp.cos(ar), jnp.cos(ac), jnp.cos(ac), zeros], axis=1)
    sin = jnp.concatenate([-jnp.sin(ar), jnp.sin(ar), -jnp.sin(ac), jnp.sin(ac), zeros], axis=1)
    return cos, sin


def _mla_mix(h, hc, w_a, g_q, g_kv, w_uq, w_ukv, w_o, w_o_sel, x, gate):
    b, n, d = h.shape
    nc = hc.shape[1]
    rope = _rope_tables(n)
    a_width = Q_LORA + KV_LORA + QK_ROPE
    a_pad = -(-a_width // _HEAD_SLOT) * _HEAD_SLOT
    w_a_pad = jnp.pad(w_a, ((0, 0), (0, a_pad - a_width))).astype(_BF16)[None]
    tm = _tile(n, 1024, _SUBLANES_BF16)
    a_l = _matmul(h, w_a_pad, tm=tm, tn=_HEAD_SLOT, out_dtype=_F32, name="mla_down_latent")
    a_c = _matmul(hc, w_a_pad[:, :, Q_LORA:], tm=_tile(nc, 1024, _SUBLANES_BF16), tn=_HEAD_SLOT,
                  out_dtype=_F32, name="mla_down_context")
    cq, ckv_l, kr_l = _mla_prep(a_l, g_q, g_kv, rope)
    ckv_c, kr_c = _mla_prep(a_c, None, g_kv, None)
    ckv = jnp.concatenate([ckv_c, ckv_l], axis=1)
    kr = jnp.concatenate([kr_c, kr_l], axis=1)

    w_uq_h = w_uq.reshape(Q_LORA, MLA_HEADS, QK_NOPE + QK_ROPE)
    w_uq_pad = jnp.pad(w_uq_h, ((0, 0), (0, 0), (0, _HEAD_SLOT - QK_NOPE - QK_ROPE)))
    w_uq_pad = w_uq_pad.reshape(Q_LORA, MLA_HEADS * _HEAD_SLOT).astype(_BF16)[None]
    q = _matmul(cq, w_uq_pad, tm=tm, tn=_tile(MLA_HEADS * _HEAD_SLOT, 1024, _HEAD_SLOT), out_dtype=_BF16,
                rope=rope, q_scale=float((QK_NOPE + QK_ROPE) ** -0.5 * _LOG2E), name="mla_q_up")
    nkv = nc + n
    w_ukv_h = w_ukv.reshape(KV_LORA, MLA_HEADS, QK_NOPE + V_HEAD)
    w_uk = w_ukv_h[:, :, :QK_NOPE].reshape(KV_LORA, MLA_HEADS * QK_NOPE).astype(_BF16)[None]
    w_uv_t = w_ukv_h[:, :, QK_NOPE:].reshape(KV_LORA, MLA_HEADS * V_HEAD).T.astype(_BF16)[None]
    k = _matmul(ckv, w_uk, tm=_tile(nkv, 1088, _SUBLANES_BF16), tn=_tile(MLA_HEADS * QK_NOPE, 2048, _LANES),
                out_dtype=_BF16, name="mla_k_up")
    vt = _matmul(w_uv_t, ckv, nt=True, tm=_tile(MLA_HEADS * V_HEAD, 1024, _SUBLANES_BF16),
                 tn=_tile(nkv, 2176, _LANES), out_dtype=_BF16, name="mla_v_up_transposed")
    o = _attention(q, k, kr, vt)
    return _matmul(o, w_o, b_sel=w_o_sel, tm=_tile(n, 512, _SUBLANES_BF16), tn=_tile(d, 512, _LANES),
                   out_dtype=_F32, resid=x, gate=gate, name="mla_out_proj")


def _ffn(x, h, w_gate, w_up, w_down, layer, gate):
    b, n, d = x.shape
    hidden = w_gate.shape[2]
    tm = _tile(n, 1024, _SUBLANES_BF16)
    u = _matmul(h, w_gate, b2=w_up, b_sel=layer, tm=tm, tn=_tile(hidden, 256, _LANES), out_dtype=_BF16,
                name="ffn_gate_up")
    half = hidden // 2
    tk = half if half % _LANES == 0 else hidden
    return _matmul(u, w_down, b_sel=layer, tm=tm, tn=_tile(d, 512, _LANES), tk=tk, out_dtype=_F32, resid=x,
                   gate=gate, name="ffn_down")


def kernel(x, c, ctx, c_ctx, w_ada, b_ada, g_mix, g_ffn, fourier_w_out, mla_w_a, mla_g_q, mla_g_kv, mla_w_uq,
           mla_w_ukv, mla_w_o, w_gate, w_up, w_down, g_final):
    b, n, d = x.shape
    depth = w_ada.shape[0]
    cond = jnp.concatenate([c, c_ctx[None]], axis=0)
    cond = jnp.pad(cond, ((0, -(b + 1) % 8), (0, 0)))
    mod = _ada_modulation(cond, w_ada, b_ada)
    w_down_bf16 = w_down.astype(_BF16)
    dg = d // FOURIER_GROUPS
    cs = jnp.stack(_dft_cos_sin(dg, dg ** -0.5))

    xc = ctx
    for i in range(depth):
        need_ctx = i < depth - 1
        j = i // 2
        lat = [mod[i, :b, s * d:(s + 1) * d].reshape(b, 1, d) for s in range(N_MOD)]
        con = [mod[i, b:b + 1, s * d:(s + 1) * d].reshape(1, 1, d) for s in range(N_MOD)]
        h = _rms_norm(x, g_mix[i], lat[1], lat[0], out_dtype=_BF16)
        hc = _rms_norm(xc, g_mix[i], con[1], con[0], out_dtype=_BF16)
        tm = _tile(n, 1024, _SUBLANES_BF16)
        if i % 2 == 0:
            x = _matmul(_fourier_2d(h, cs), fourier_w_out, b_sel=j, tm=tm, tn=_tile(d, 256, _LANES),
                        out_dtype=_F32, resid=x, gate=lat[2], name="fourier_out_proj")
            if need_ctx:
                xc = _matmul(_fourier_2d(hc, cs), fourier_w_out, b_sel=j,
                             tm=_tile(xc.shape[1], 1024, _SUBLANES_BF16), tn=_tile(d, 256, _LANES),
                             out_dtype=_F32, resid=xc, gate=con[2], name="fourier_out_proj_ctx")
        else:
            assert not need_ctx, "context queries are only implemented for the Fourier mixer layers"
            x = _mla_mix(h, hc, mla_w_a[j], mla_g_q[j], mla_g_kv[j], mla_w_uq[j], mla_w_ukv[j],
                         mla_w_o.astype(_BF16), j, x, lat[2])
        x = _ffn(x, _rms_norm(x, g_ffn[i], lat[4], lat[3], out_dtype=_BF16), w_gate, w_up, w_down_bf16, i,
                 lat[5])
        if need_ctx:
            xc = _ffn(xc, _rms_norm(xc, g_ffn[i], con[4], con[3], out_dtype=_BF16), w_gate, w_up, w_down_bf16,
                      i, con[5])
    return _rms_norm(x, g_final, out_dtype=x.dtype)
```

```python
import functools
import math

import jax
import jax.numpy as jnp
import numpy as np
from jax import lax
from jax.experimental import pallas as pl
from jax.experimental.pallas import tpu as pltpu

_F32 = jnp.float32
_BF16 = jnp.bfloat16

GRID_W = 64
FOURIER_GROUPS = 4
MLA_HEADS = 64
Q_LORA = 1536
KV_LORA = 512
QK_NOPE = 128
QK_ROPE = 64
V_HEAD = 128
ROPE_BASE = 10000.0
EPS = 1e-6
N_MOD = 6

_LANES = 128
_SUBLANES_BF16 = 16
_VMEM_CAP_BYTES = 56 * 1024 * 1024
_HEAD_SLOT = 2 * _LANES
_ATTN_KEY_CHUNK = 512
_ATTN_LOOKAHEAD = 4
_LOG2E = math.log2(math.e)


def _tile(dim, pref, align):
    if dim <= pref:
        return dim
    t = (pref // align) * align
    while t >= align:
        if dim % t == 0:
            return t
        t -= align
    return dim


def _nbytes(shape, dtype):
    return int(np.prod(shape)) * jnp.dtype(dtype).itemsize


def _params(semantics, vmem_bytes):
    return pltpu.CompilerParams(dimension_semantics=semantics,
                                vmem_limit_bytes=int(min(_VMEM_CAP_BYTES, vmem_bytes)))


def _ada_body(c_ref, w_ref, b_ref, o_ref):
    c = c_ref[...]
    s = c * jax.nn.sigmoid(c)
    acc = jnp.dot(s.astype(_BF16), w_ref[...].astype(_BF16), preferred_element_type=_F32)
    o_ref[...] = acc + b_ref[...]


def _ada_modulation(cc, w_ada, b_ada):
    depth, d, n = w_ada.shape
    r = cc.shape[0]
    tn = _tile(n, 512, _LANES)
    vmem = 2 * _nbytes((d, tn), _F32) + _nbytes((d, tn), _BF16) + 8 * _nbytes((r, tn), _F32) + (4 << 20)
    return pl.pallas_call(
        _ada_body,
        grid=(depth, n // tn),
        in_specs=[
            pl.BlockSpec((r, d), lambda l, j: (0, 0)),
            pl.BlockSpec((None, d, tn), lambda l, j: (l, 0, j)),
            pl.BlockSpec((None, 1, tn), lambda l, j: (l, 0, j)),
        ],
        out_specs=pl.BlockSpec((None, r, tn), lambda l, j: (l, 0, j)),
        out_shape=jax.ShapeDtypeStruct((depth, r, n), _F32),
        compiler_params=_params(("parallel", "parallel"), vmem),
        name="ada_modulation",
    )(cc, w_ada, b_ada.reshape(depth, 1, n))


def _norm_body(*refs, modulate):
    if modulate:
        x_ref, g_ref, sc_ref, sh_ref, o_ref = refs
    else:
        x_ref, g_ref, o_ref = refs
    x = x_ref[...]
    y = x * lax.rsqrt(jnp.mean(x * x, axis=-1, keepdims=True) + EPS) * g_ref[...]
    if modulate:
        y = y * (1.0 + sc_ref[...]) + sh_ref[...]
    o_ref[...] = y.astype(o_ref.dtype)


def _rms_norm(x, g, scale=None, shift=None, *, out_dtype):
    b, n, d = x.shape
    tr = _tile(n, 256, 8)
    modulate = scale is not None
    in_specs = [pl.BlockSpec((None, tr, d), lambda bb, i: (bb, i, 0)),
                pl.BlockSpec((1, d), lambda bb, i: (0, 0))]
    args = [x, g.reshape(1, d)]
    if modulate:
        per_batch = scale.shape[0] > 1
        mod_spec = pl.BlockSpec((None, 1, d), (lambda bb, i: (bb, 0, 0)) if per_batch else (lambda bb, i: (0, 0, 0)))
        in_specs += [mod_spec, mod_spec]
        args += [scale, shift]
    vmem = 2 * _nbytes((tr, d), _F32) + 2 * _nbytes((tr, d), out_dtype) + 3 * _nbytes((tr, d), _F32) + (2 << 20)
    return pl.pallas_call(
        functools.partial(_norm_body, modulate=modulate),
        grid=(b, n // tr),
        in_specs=in_specs,
        out_specs=pl.BlockSpec((None, tr, d), lambda bb, i: (bb, i, 0)),
        out_shape=jax.ShapeDtypeStruct((b, n, d), out_dtype),
        compiler_params=_params(("parallel", "parallel"), vmem),
        name="rms_norm_modulate" if modulate else "rms_norm",
    )(*args)


def _rope_rotate(t, cos, sin, upper_half):
    lane = lax.broadcasted_iota(jnp.int32, t.shape, 1)
    quarter = QK_ROPE // 4
    even_chunk = (lane // quarter) % 2 == 0
    swapped = jnp.where(even_chunk, pltpu.roll(t, _LANES - quarter, 1), pltpu.roll(t, quarter, 1))
    valid = (lane >= QK_ROPE) if upper_half else (lane < QK_ROPE)
    return jnp.where(valid, t * cos + swapped * sin, 0.0)


def _q_head_slots(acc, cos, sin, q_scale):
    tiles = [acc[:, i * _LANES:(i + 1) * _LANES] for i in range(acc.shape[1] // _LANES)]
    low_half = lax.broadcasted_iota(jnp.int32, tiles[0].shape, 1) < QK_ROPE
    cos_lo, cos_hi = cos[:, :_LANES], cos[:, _LANES:]
    sin_lo, sin_hi = sin[:, :_LANES], sin[:, _LANES:]
    parts = []
    for p in range(len(tiles) // 3):
        t0, t1, t2 = tiles[3 * p:3 * p + 3]
        parts += [t0 * q_scale, _rope_rotate(t1, cos_lo, sin_lo, False) * q_scale,
                  jnp.where(low_half, t2, t1) * q_scale, _rope_rotate(t2, cos_hi, sin_hi, True) * q_scale]
    return jnp.concatenate(parts, axis=1)


def _mm_body(*refs, n_b, epilogue, nk, q_scale, nt):
    a_ref = refs[0]
    b_refs = refs[1:1 + n_b]
    pos = 1 + n_b
    extra = ()
    if epilogue in ("resgate", "qrope"):
        extra = refs[pos:pos + 2]
        pos += 2
    o_ref = refs[pos]
    acc_refs = refs[pos + 1:]

    def finish(accs):
        if epilogue == "plain":
            out = accs[0]
        elif epilogue == "swiglu":
            g, u = accs
            out = g * jax.nn.sigmoid(g) * u
        elif epilogue == "resgate":
            res_ref, gate_ref = extra
            out = res_ref[...] + gate_ref[...] * accs[0]
        elif epilogue == "qrope":
            cos_ref, sin_ref = extra
            out = _q_head_slots(accs[0], cos_ref[...], sin_ref[...], q_scale)
        o_ref[...] = out.astype(o_ref.dtype)

    def dots():
        a = a_ref[...]
        outs = []
        for b_ref in b_refs:
            b = b_ref[...].astype(_BF16)
            if nt:
                outs.append(lax.dot_general(a, b, (((1,), (1,)), ((), ())), preferred_element_type=_F32))
            else:
                outs.append(jnp.dot(a, b, preferred_element_type=_F32))
        return outs

    if nk == 1:
        finish(dots())
        return

    k = pl.program_id(3)

    @pl.when(k == 0)
    def _():
        for acc_ref, d in zip(acc_refs, dots()):
            acc_ref[...] = d

    @pl.when(k > 0)
    def _():
        for acc_ref, d in zip(acc_refs, dots()):
            acc_ref[...] += d

    @pl.when(k == nk - 1)
    def _():
        finish([acc_ref[...] for acc_ref in acc_refs])


def _matmul(a, b, *, tm, tn, tk=None, out_dtype, b2=None, b_sel=None, nt=False, resid=None, gate=None,
            rope=None, q_scale=1.0, name):
    ba, m, kdim = a.shape
    lb = b.shape[0]
    n = b.shape[1] if nt else b.shape[2]
    b_batched = b_sel is None and lb > 1
    bt = max(ba, lb if b_batched else 1, resid.shape[0] if resid is not None else 1)
    tk = kdim if tk is None else tk
    nk = kdim // tk
    assert m % tm == 0 and n % tn == 0 and kdim % tk == 0, (name, a.shape, b.shape, tm, tn, tk)

    def bidx(count):
        return (lambda z: z) if count > 1 else (lambda z: 0)

    ai = bidx(ba)
    if b_sel is not None:
        bi = lambda z: b_sel
    else:
        bi = bidx(lb)
    if nt:
        b_spec = pl.BlockSpec((None, tn, tk), lambda z, i, j, k: (bi(z), j, k))
        b_block = (tn, tk)
    else:
        b_spec = pl.BlockSpec((None, tk, tn), lambda z, i, j, k: (bi(z), k, j))
        b_block = (tk, tn)
    in_specs = [pl.BlockSpec((None, tm, tk), lambda z, i, j, k: (ai(z), i, k)), b_spec]
    args = [a, b]
    n_b = 1
    epilogue = "plain"
    blocks = [((tm, tk), a.dtype), (b_block, b.dtype)]
    if b2 is not None:
        in_specs.append(b_spec)
        args.append(b2)
        blocks.append((b_block, b2.dtype))
        n_b = 2
        epilogue = "swiglu"
    if resid is not None:
        gi = bidx(gate.shape[0])
        in_specs += [pl.BlockSpec((None, tm, tn), lambda z, i, j, k: (z, i, j)),
                     pl.BlockSpec((None, 1, tn), lambda z, i, j, k: (gi(z), 0, j))]
        args += [resid, gate]
        blocks.append(((tm, tn), resid.dtype))
        epilogue = "resgate"
    out_tn, out_n = tn, n
    if rope is not None:
        head = QK_NOPE + QK_ROPE
        assert tn % (2 * head) == 0 and nk == 1
        out_tn, out_n = tn // head * _HEAD_SLOT, n // head * _HEAD_SLOT
        table_spec = pl.BlockSpec((tm, _HEAD_SLOT), lambda z, i, j, k: (i, 0))
        in_specs += [table_spec, table_spec]
        args += list(rope)
        blocks += [((tm, _HEAD_SLOT), _F32)] * 2
        epilogue = "qrope"
    blocks.append(((tm, out_tn), out_dtype))
    scratch = [pltpu.VMEM((tm, tn), _F32) for _ in range(n_b)] if nk > 1 else []
    cast_tmp = n_b * _nbytes(b_block, _BF16) if b.dtype != _BF16 else 0
    vmem = (2 * sum(_nbytes(s, dt) for s, dt in blocks) + cast_tmp + (n_b + 2) * _nbytes((tm, tn), _F32)
            + (2 << 20))
    return pl.pallas_call(
        functools.partial(_mm_body, n_b=n_b, epilogue=epilogue, nk=nk, q_scale=q_scale, nt=nt),
        grid=(bt, m // tm, n // tn, nk),
        in_specs=in_specs,
        out_specs=pl.BlockSpec((None, tm, out_tn), lambda z, i, j, k: (z, i, j)),
        out_shape=jax.ShapeDtypeStruct((bt, m, out_n), out_dtype),
        scratch_shapes=scratch,
        compiler_params=_params(("parallel", "parallel", "parallel", "arbitrary"), vmem),
        name=name,
    )(*args)


def _dft_cos_sin(n, scale):
    idx = jnp.arange(n, dtype=jnp.int32)
    phase = (idx[:, None] * idx[None, :]) % n
    ang = phase.astype(_F32) * (2.0 * np.pi / n)
    return (jnp.cos(ang) * scale).astype(_BF16), (jnp.sin(ang) * scale).astype(_BF16)


def _dft_gen_body(ca_ref, sa_ref, cb_ref, sb_ref, o_ref, *, n, scale):
    ca, sa, cb, sb = ca_ref[...], sa_ref[...], cb_ref[...], sb_ref[...]
    o_ref[:, :n] = ((ca * cb - sa * sb) * scale).astype(o_ref.dtype)
    o_ref[:, n:] = ((sa * cb + ca * sb) * -scale).astype(o_ref.dtype)


def _position_dft_matrix(n):
    tr = _tile(n, 256, _SUBLANES_BF16)
    t = jnp.arange(n, dtype=jnp.int32)[None, :]
    two_pi_n = 2.0 * np.pi / n
    beta = ((jnp.arange(tr, dtype=jnp.int32)[:, None] * t) % n).astype(_F32) * two_pi_n
    alpha = ((jnp.arange(0, n, tr, dtype=jnp.int32)[:, None] * t) % n).astype(_F32) * two_pi_n
    alpha = alpha.reshape(n // tr, 1, n)
    row_spec = pl.BlockSpec((None, 1, n), lambda i: (i, 0, 0))
    base_spec = pl.BlockSpec((tr, n), lambda i: (0, 0))
    vmem = 4 * _nbytes((tr, n), _F32) + 2 * _nbytes((tr, 2 * n), _BF16) + 4 * _nbytes((tr, n), _F32) + (2 << 20)
    return pl.pallas_call(
        functools.partial(_dft_gen_body, n=n, scale=float(n ** -0.5)),
        grid=(n // tr,),
        in_specs=[row_spec, row_spec, base_spec, base_spec],
        out_specs=pl.BlockSpec((tr, 2 * n), lambda i: (i, 0)),
        out_shape=jax.ShapeDtypeStruct((n, 2 * n), _BF16),
        compiler_params=_params(("parallel",), vmem),
        name="fourier_position_matrix",
    )(jnp.cos(alpha), jnp.sin(alpha), jnp.cos(beta), jnp.sin(beta))


def _channel_dft(h, cs):
    b, n, d = h.shape
    dg = cs.shape[1]
    groups = d // dg
    tm = _tile(n, 1024, _SUBLANES_BF16)
    vmem = 2 * (_nbytes((tm, dg), _BF16) * 2 + _nbytes((dg, dg), _BF16)) + 2 * _nbytes((tm, dg), _F32) + (2 << 20)
    return pl.pallas_call(
        functools.partial(_mm_body, n_b=1, epilogue="plain", nk=1, q_scale=1.0, nt=False),
        grid=(b, n // tm, groups, 2),
        in_specs=[pl.BlockSpec((None, tm, dg), lambda z, i, g, c: (z, i, g)),
                  pl.BlockSpec((None, dg, dg), lambda z, i, g, c: (c, 0, 0))],
        out_specs=pl.BlockSpec((None, None, tm, dg), lambda z, i, g, c: (z, c, i, g)),
        out_shape=jax.ShapeDtypeStruct((b, 2, n, d), _BF16),
        compiler_params=_params(("parallel",) * 4, vmem),
        name="fourier_channel_dft",
    )(h, cs)


def _fourier_2d(h, cs):
    b, n, d = h.shape
    z = _channel_dft(h, cs)
    a = _position_dft_matrix(n)[None]
    return _matmul(a, z.reshape(b, 2 * n, d), tm=_tile(n, 1024, _SUBLANES_BF16), tn=_tile(d, 1024, _LANES),
                   tk=_tile(2 * n, 2048, _LANES), out_dtype=_BF16, name="fourier_position_dft")


def _mla_prep_body(*refs, with_q, with_rope):
    refs = list(refs)
    a_ref = refs.pop(0)
    gq_ref = refs.pop(0) if with_q else None
    gkv_ref = refs.pop(0)
    cos_ref, sin_ref = (refs.pop(0), refs.pop(0)) if with_rope else (None, None)
    cq_ref = refs.pop(0) if with_q else None
    ckv_ref, kr_ref = refs

    def norm(t, g):
        return t * lax.rsqrt(jnp.mean(t * t, axis=-1, keepdims=True) + EPS) * g

    off = 0
    if with_q:
        cq_ref[...] = norm(a_ref[:, :Q_LORA], gq_ref[...]).astype(cq_ref.dtype)
        off = Q_LORA
    ckv_ref[...] = norm(a_ref[:, off:off + KV_LORA], gkv_ref[...]).astype(ckv_ref.dtype)
    kr = a_ref[:, off + KV_LORA:off + KV_LORA + _LANES]
    if with_rope:
        kr = _rope_rotate(kr, cos_ref[:, :_LANES], sin_ref[:, :_LANES], False)
    kr_ref[:, :_LANES] = kr.astype(kr_ref.dtype)
    kr_ref[:, _LANES:] = pltpu.roll(kr, QK_ROPE, 1).astype(kr_ref.dtype)


def _mla_prep(a, g_q, g_kv, rope):
    b, n, w = a.shape
    with_q = g_q is not None
    with_rope = rope is not None
    tr = _tile(n, 256, _SUBLANES_BF16)
    row = lambda z, i: (z, i, 0)
    in_specs = [pl.BlockSpec((None, tr, w), row)]
    args = [a]
    if with_q:
        in_specs.append(pl.BlockSpec((1, Q_LORA), lambda z, i: (0, 0)))
        args.append(g_q.reshape(1, Q_LORA))
    in_specs.append(pl.BlockSpec((1, KV_LORA), lambda z, i: (0, 0)))
    args.append(g_kv.reshape(1, KV_LORA))
    if with_rope:
        in_specs += [pl.BlockSpec((tr, _HEAD_SLOT), lambda z, i: (i, 0))] * 2
        args += list(rope)
    out_specs, out_shape = [], []
    if with_q:
        out_specs.append(pl.BlockSpec((None, tr, Q_LORA), row))
        out_shape.append(jax.ShapeDtypeStruct((b, n, Q_LORA), _BF16))
    out_specs += [pl.BlockSpec((None, tr, KV_LORA), row), pl.BlockSpec((None, tr, _HEAD_SLOT), row)]
    out_shape += [jax.ShapeDtypeStruct((b, n, KV_LORA), _BF16), jax.ShapeDtypeStruct((b, n, _HEAD_SLOT), _BF16)]
    vmem = 6 * _nbytes((tr, w), _F32) + (2 << 20)
    return pl.pallas_call(
        functools.partial(_mla_prep_body, with_q=with_q, with_rope=with_rope),
        grid=(b, n // tr),
        in_specs=in_specs,
        out_specs=out_specs,
        out_shape=out_shape,
        compiler_params=_params(("parallel", "parallel"), vmem),
        name="mla_prep_latent" if with_q else "mla_prep_context",
    )(*args)


def _attn_body(q_ref, k_ref, kr_ref, vt_ref, o_ref, kcat_ref, *, chunks):
    @pl.when(pl.program_id(2) == 0)
    def _():
        kcat_ref[:, :QK_NOPE] = k_ref[...]
        kcat_ref[:, QK_NOPE:] = kr_ref[...]

    q = q_ref[...]

    def scores(start, size):
        return lax.dot_general(kcat_ref[start:start + size, :], q, (((1,), (1,)), ((), ())),
                               preferred_element_type=_F32)

    starts = [sum(chunks[:c]) for c in range(len(chunks))]
    m = l = acc = None
    ahead = [scores(starts[c], chunks[c]) for c in range(min(_ATTN_LOOKAHEAD, len(chunks)))]
    for c, (start, size) in enumerate(zip(starts, chunks)):
        st = ahead.pop(0)
        nxt = c + _ATTN_LOOKAHEAD
        if nxt < len(chunks):
            ahead.append(scores(starts[nxt], chunks[nxt]))
        mc = jnp.max(st, axis=0, keepdims=True)
        m_new = mc if m is None else jnp.maximum(m, mc)
        p = jnp.exp2(st - m_new)
        lc = jnp.sum(p, axis=0, keepdims=True)
        pv = jnp.dot(vt_ref[:, start:start + size], p.astype(_BF16), preferred_element_type=_F32)
        if m is None:
            l, acc = lc, pv
        else:
            alpha = jnp.exp2(m - m_new)
            l = alpha * l + lc
            acc = alpha * acc + pv
        m = m_new
    o_ref[...] = (acc / l).T.astype(o_ref.dtype)


def _attention(q, k, kr, vt):
    b, n, _ = q.shape
    nk = k.shape[1]
    tq = _tile(n, 1024, _LANES)
    chunk = _ATTN_KEY_CHUNK
    chunks = [chunk] * (nk // chunk) + ([nk % chunk] if nk % chunk else [])
    vmem = (4 * _nbytes((tq, _HEAD_SLOT), _BF16) + 5 * _nbytes((nk, _LANES), _BF16)
            + 2 * _nbytes((V_HEAD, nk), _BF16) + 6 * _nbytes((min(chunk, nk), tq), _F32) + (4 << 20))
    return pl.pallas_call(
        functools.partial(_attn_body, chunks=chunks),
        grid=(b, MLA_HEADS, n // tq),
        in_specs=[pl.BlockSpec((None, tq, _HEAD_SLOT), lambda z, h, i: (z, i, h)),
                  pl.BlockSpec((None, nk, QK_NOPE), lambda z, h, i: (z, 0, h)),
                  pl.BlockSpec((None, nk, _LANES), lambda z, h, i: (z, 0, h % 2)),
                  pl.BlockSpec((None, V_HEAD, nk), lambda z, h, i: (z, h, 0))],
        out_specs=pl.BlockSpec((None, tq, V_HEAD), lambda z, h, i: (z, i, h)),
        out_shape=jax.ShapeDtypeStruct((b, n, MLA_HEADS * V_HEAD), _BF16),
        scratch_shapes=[pltpu.VMEM((nk, _HEAD_SLOT), _BF16)],
        compiler_params=_params(("parallel", "parallel", "arbitrary"), vmem),
        name="mla_attention",
    )(q, k, kr, vt)


def _rope_tables(n):
    rows_n = n // GRID_W
    row = jnp.repeat(jnp.arange(rows_n, dtype=_F32), GRID_W)
    col = jnp.tile(jnp.arange(GRID_W, dtype=_F32), rows_n)
    n_freq = QK_ROPE // 4
    inv_freq = ROPE_BASE ** (-jnp.arange(n_freq, dtype=_F32) / n_freq)
    ar, ac = row[:, None] * inv_freq, col[:, None] * inv_freq
    zeros = jnp.zeros((n, _LANES - QK_ROPE), _F32)
    cos = jnp.concatenate([jnp.cos(ar), jnp.cos(ar), jnp.cos(ac), jnp.cos(ac)], axis=1)
    sin = jnp.concatenate([-jnp.sin(ar), jnp.sin(ar), -jnp.sin(ac), jnp.sin(ac)], axis=1)
    return (jnp.concatenate([cos, zeros, zeros, cos], axis=1), jnp.concatenate([sin, zeros, zeros, sin], axis=1))


def _mla_mix(h, hc, w_a, g_q, g_kv, w_uq, w_ukv, w_o, w_o_sel, x, gate):
    b, n, d = h.shape
    nc = hc.shape[1]
    rope = _rope_tables(n)
    a_width = Q_LORA + KV_LORA + QK_ROPE
    a_pad = -(-a_width // _HEAD_SLOT) * _HEAD_SLOT
    w_a_pad = jnp.pad(w_a, ((0, 0), (0, a_pad - a_width))).astype(_BF16)[None]
    tm = _tile(n, 1024, _SUBLANES_BF16)
    a_l = _matmul(h, w_a_pad, tm=tm, tn=_HEAD_SLOT, out_dtype=_F32, name="mla_down_latent")
    a_c = _matmul(hc, w_a_pad[:, :, Q_LORA:], tm=_tile(nc, 1024, _SUBLANES_BF16), tn=_HEAD_SLOT,
                  out_dtype=_F32, name="mla_down_context")
    cq, ckv_l, kr_l = _mla_prep(a_l, g_q, g_kv, rope)
    ckv_c, kr_c = _mla_prep(a_c, None, g_kv, None)
    ckv = jnp.concatenate([ckv_c, ckv_l], axis=1)
    kr = jnp.concatenate([kr_c, kr_l], axis=1)

    head = QK_NOPE + QK_ROPE
    q = _matmul(cq, w_uq[None], tm=tm, tn=_tile(MLA_HEADS * head, 4 * head, 2 * head), out_dtype=_BF16,
                rope=rope, q_scale=float(head ** -0.5 * _LOG2E), name="mla_q_up")
    nkv = nc + n
    w_ukv_h = w_ukv.reshape(KV_LORA, MLA_HEADS // 2, 2, QK_NOPE + V_HEAD)
    w_uk_h = w_ukv_h[..., :QK_NOPE]
    w_uk_h = jnp.stack([w_uk_h[:, :, 0], jnp.roll(w_uk_h[:, :, 1], -(QK_NOPE // 2), axis=-1)], axis=2)
    w_uk = w_uk_h.reshape(KV_LORA, MLA_HEADS * QK_NOPE).astype(_BF16)[None]
    w_ukv_h = w_ukv_h.reshape(KV_LORA, MLA_HEADS, QK_NOPE + V_HEAD)
    w_uv_t = w_ukv_h[:, :, QK_NOPE:].reshape(KV_LORA, MLA_HEADS * V_HEAD).T.astype(_BF16)[None]
    k = _matmul(ckv, w_uk, tm=_tile(nkv, 1088, _SUBLANES_BF16), tn=_tile(MLA_HEADS * QK_NOPE, 2048, _LANES),
                out_dtype=_BF16, name="mla_k_up")
    vt = _matmul(w_uv_t, ckv, nt=True, tm=_tile(MLA_HEADS * V_HEAD, 1024, _SUBLANES_BF16),
                 tn=_tile(nkv, 2176, _LANES), out_dtype=_BF16, name="mla_v_up_transposed")
    o = _attention(q, k, kr, vt)
    return _matmul(o, w_o, b_sel=w_o_sel, tm=_tile(n, 512, _SUBLANES_BF16), tn=_tile(d, 512, _LANES),
                   out_dtype=_F32, resid=x, gate=gate, name="mla_out_proj")


def _ffn(x, h, w_gate, w_up, w_down, layer, gate):
    b, n, d = x.shape
    hidden = w_gate.shape[2]
    tm = _tile(n, 1024, _SUBLANES_BF16)
    u = _matmul(h, w_gate, b2=w_up, b_sel=layer, tm=tm, tn=_tile(hidden, 256, _LANES), out_dtype=_BF16,
                name="ffn_gate_up")
    half = hidden // 2
    tk = half if half % _LANES == 0 else hidden
    return _matmul(u, w_down, b_sel=layer, tm=tm, tn=_tile(d, 512, _LANES), tk=tk, out_dtype=_F32, resid=x,
                   gate=gate, name="ffn_down")


def kernel(x, c, ctx, c_ctx, w_ada, b_ada, g_mix, g_ffn, fourier_w_out, mla_w_a, mla_g_q, mla_g_kv, mla_w_uq,
           mla_w_ukv, mla_w_o, w_gate, w_up, w_down, g_final):
    b, n, d = x.shape
    depth = w_ada.shape[0]
    cond = jnp.concatenate([c, c_ctx[None]], axis=0)
    cond = jnp.pad(cond, ((0, -(b + 1) % 8), (0, 0)))
    mod = _ada_modulation(cond, w_ada, b_ada)
    w_down_bf16 = w_down.astype(_BF16)
    dg = d // FOURIER_GROUPS
    cs = jnp.stack(_dft_cos_sin(dg, dg ** -0.5))

    xc = ctx
    for i in range(depth):
        need_ctx = i < depth - 1
        j = i // 2
        lat = [mod[i, :b, s * d:(s + 1) * d].reshape(b, 1, d) for s in range(N_MOD)]
        con = [mod[i, b:b + 1, s * d:(s + 1) * d].reshape(1, 1, d) for s in range(N_MOD)]
        h = _rms_norm(x, g_mix[i], lat[1], lat[0], out_dtype=_BF16)
        hc = _rms_norm(xc, g_mix[i], con[1], con[0], out_dtype=_BF16)
        tm = _tile(n, 1024, _SUBLANES_BF16)
        if i % 2 == 0:
            x = _matmul(_fourier_2d(h, cs), fourier_w_out, b_sel=j, tm=tm, tn=_tile(d, 256, _LANES),
                        out_dtype=_F32, resid=x, gate=lat[2], name="fourier_out_proj")
            if need_ctx:
                xc = _matmul(_fourier_2d(hc, cs), fourier_w_out, b_sel=j,
                             tm=_tile(xc.shape[1], 1024, _SUBLANES_BF16), tn=_tile(d, 256, _LANES),
                             out_dtype=_F32, resid=xc, gate=con[2], name="fourier_out_proj_ctx")
        else:
            assert not need_ctx, "context queries are only implemented for the Fourier mixer layers"
            x = _mla_mix(h, hc, mla_w_a[j], mla_g_q[j], mla_g_kv[j], mla_w_uq[j], mla_w_ukv[j],
                         mla_w_o.astype(_BF16), j, x, lat[2])
        x = _ffn(x, _rms_norm(x, g_ffn[i], lat[4], lat[3], out_dtype=_BF16), w_gate, w_up, w_down_bf16, i,
                 lat[5])
        if need_ctx:
            xc = _ffn(xc, _rms_norm(xc, g_ffn[i], con[4], con[3], out_dtype=_BF16), w_gate, w_up, w_down_bf16,
                      i, con[5])
    return _rms_norm(x, g_final, out_dtype=x.dtype)
```

```python
import functools
import math

import jax
import jax.numpy as jnp
import numpy as np
from jax import lax
from jax.experimental import pallas as pl
from jax.experimental.pallas import tpu as pltpu

_F32 = jnp.float32
_BF16 = jnp.bfloat16

GRID_W = 64
FOURIER_GROUPS = 4
MLA_HEADS = 64
Q_LORA = 1536
KV_LORA = 512
QK_NOPE = 128
QK_ROPE = 64
V_HEAD = 128
ROPE_BASE = 10000.0
EPS = 1e-6
N_MOD = 6

_LANES = 128
_SUBLANES_BF16 = 16
_VMEM_CAP_BYTES = 56 * 1024 * 1024
_HEAD_SLOT = 2 * _LANES
_ATTN_KEY_CHUNK = 512
_ATTN_LOOKAHEAD = 2
_LOG2E = math.log2(math.e)


def _tile(dim, pref, align):
    if dim <= pref:
        return dim
    t = (pref // align) * align
    while t >= align:
        if dim % t == 0:
            return t
        t -= align
    return dim


def _nbytes(shape, dtype):
    return int(np.prod(shape)) * jnp.dtype(dtype).itemsize


def _params(semantics, vmem_bytes):
    return pltpu.CompilerParams(dimension_semantics=semantics,
                                vmem_limit_bytes=int(min(_VMEM_CAP_BYTES, vmem_bytes)))


def _ada_body(c_ref, w_ref, b_ref, o_ref):
    c = c_ref[...]
    s = c * jax.nn.sigmoid(c)
    acc = jnp.dot(s.astype(_BF16), w_ref[...].astype(_BF16), preferred_element_type=_F32)
    o_ref[...] = acc + b_ref[...]


def _ada_modulation(cc, w_ada, b_ada):
    depth, d, n = w_ada.shape
    r = cc.shape[0]
    tn = _tile(n, 512, _LANES)
    vmem = 2 * _nbytes((d, tn), _F32) + _nbytes((d, tn), _BF16) + 8 * _nbytes((r, tn), _F32) + (4 << 20)
    return pl.pallas_call(
        _ada_body,
        grid=(depth, n // tn),
        in_specs=[
            pl.BlockSpec((r, d), lambda l, j: (0, 0)),
            pl.BlockSpec((None, d, tn), lambda l, j: (l, 0, j)),
            pl.BlockSpec((None, 1, tn), lambda l, j: (l, 0, j)),
        ],
        out_specs=pl.BlockSpec((None, r, tn), lambda l, j: (l, 0, j)),
        out_shape=jax.ShapeDtypeStruct((depth, r, n), _F32),
        compiler_params=_params(("parallel", "parallel"), vmem),
        name="ada_modulation",
    )(cc, w_ada, b_ada.reshape(depth, 1, n))


def _norm_body(*refs, modulate):
    if modulate:
        x_ref, g_ref, sc_ref, sh_ref, o_ref = refs
    else:
        x_ref, g_ref, o_ref = refs
    x = x_ref[...]
    y = x * lax.rsqrt(jnp.mean(x * x, axis=-1, keepdims=True) + EPS) * g_ref[...]
    if modulate:
        y = y * (1.0 + sc_ref[...]) + sh_ref[...]
    o_ref[...] = y.astype(o_ref.dtype)


def _rms_norm(x, g, scale=None, shift=None, *, out_dtype):
    b, n, d = x.shape
    tr = _tile(n, 256, 8)
    modulate = scale is not None
    in_specs = [pl.BlockSpec((None, tr, d), lambda bb, i: (bb, i, 0)),
                pl.BlockSpec((1, d), lambda bb, i: (0, 0))]
    args = [x, g.reshape(1, d)]
    if modulate:
        per_batch = scale.shape[0] > 1
        mod_spec = pl.BlockSpec((None, 1, d), (lambda bb, i: (bb, 0, 0)) if per_batch else (lambda bb, i: (0, 0, 0)))
        in_specs += [mod_spec, mod_spec]
        args += [scale, shift]
    vmem = 2 * _nbytes((tr, d), _F32) + 2 * _nbytes((tr, d), out_dtype) + 3 * _nbytes((tr, d), _F32) + (2 << 20)
    return pl.pallas_call(
        functools.partial(_norm_body, modulate=modulate),
        grid=(b, n // tr),
        in_specs=in_specs,
        out_specs=pl.BlockSpec((None, tr, d), lambda bb, i: (bb, i, 0)),
        out_shape=jax.ShapeDtypeStruct((b, n, d), out_dtype),
        compiler_params=_params(("parallel", "parallel"), vmem),
        name="rms_norm_modulate" if modulate else "rms_norm",
    )(*args)


def _rope_rotate(t, cos, sin, upper_half):
    lane = lax.broadcasted_iota(jnp.int32, t.shape, 1)
    quarter = QK_ROPE // 4
    even_chunk = (lane // quarter) % 2 == 0
    swapped = jnp.where(even_chunk, pltpu.roll(t, _LANES - quarter, 1), pltpu.roll(t, quarter, 1))
    valid = (lane >= QK_ROPE) if upper_half else (lane < QK_ROPE)
    return jnp.where(valid, t * cos + swapped * sin, 0.0)


def _q_head_slots(acc, cos, sin, q_scale):
    tiles = [acc[:, i * _LANES:(i + 1) * _LANES] for i in range(acc.shape[1] // _LANES)]
    low_half = lax.broadcasted_iota(jnp.int32, tiles[0].shape, 1) < QK_ROPE
    cos_lo, cos_hi = cos[:, :_LANES], cos[:, _LANES:]
    sin_lo, sin_hi = sin[:, :_LANES], sin[:, _LANES:]
    parts = []
    for p in range(len(tiles) // 3):
        t0, t1, t2 = tiles[3 * p:3 * p + 3]
        parts += [t0 * q_scale, _rope_rotate(t1, cos_lo, sin_lo, False) * q_scale,
                  jnp.where(low_half, t2, t1) * q_scale, _rope_rotate(t2, cos_hi, sin_hi, True) * q_scale]
    return jnp.concatenate(parts, axis=1)


def _side_cast_blocks(rows, cols, steps):
    widths = [c for c in range(_LANES, cols + 1, _LANES) if cols % c == 0]
    for bc in sorted(widths, key=lambda c: abs(c - 4 * _LANES)):
        ncb = cols // bc
        if steps % ncb == 0 and rows % (steps // ncb) == 0 and (rows // (steps // ncb)) % _SUBLANES_BF16 == 0:
            return rows // (steps // ncb), bc
    raise ValueError(f"no tile-aligned split of {(rows, cols)} into {steps} blocks")


def _mm_body(*refs, n_b, epilogue, nk, q_scale, nt, side):
    a_ref = refs[0]
    b_refs = refs[1:1 + n_b]
    pos = 1 + n_b
    extra = ()
    if epilogue in ("resgate", "qrope"):
        extra = refs[pos:pos + 2]
        pos += 2
    if side:
        refs[pos + 2][...] = refs[pos][...].astype(refs[pos + 2].dtype)
        refs = refs[:pos] + refs[pos + 1:pos + 2] + refs[pos + 3:]
    o_ref = refs[pos]
    acc_refs = refs[pos + 1:]

    def finish(accs):
        if epilogue == "plain":
            out = accs[0]
        elif epilogue == "swiglu":
            g, u = accs
            out = g * jax.nn.sigmoid(g) * u
        elif epilogue == "resgate":
            res_ref, gate_ref = extra
            out = res_ref[...] + gate_ref[...] * accs[0]
        elif epilogue == "qrope":
            cos_ref, sin_ref = extra
            out = _q_head_slots(accs[0], cos_ref[...], sin_ref[...], q_scale)
        o_ref[...] = out.astype(o_ref.dtype)

    def dots():
        a = a_ref[...]
        outs = []
        for b_ref in b_refs:
            b = b_ref[...].astype(_BF16)
            if nt:
                outs.append(lax.dot_general(a, b, (((1,), (1,)), ((), ())), preferred_element_type=_F32))
            else:
                outs.append(jnp.dot(a, b, preferred_element_type=_F32))
        return outs

    if nk == 1:
        finish(dots())
        return

    k = pl.program_id(3)

    @pl.when(k == 0)
    def _():
        for acc_ref, d in zip(acc_refs, dots()):
            acc_ref[...] = d

    @pl.when(k > 0)
    def _():
        for acc_ref, d in zip(acc_refs, dots()):
            acc_ref[...] += d

    @pl.when(k == nk - 1)
    def _():
        finish([acc_ref[...] for acc_ref in acc_refs])


def _matmul(a, b, *, tm, tn, tk=None, out_dtype, b2=None, b_sel=None, nt=False, resid=None, gate=None,
            rope=None, q_scale=1.0, side_cast=None, name):
    ba, m, kdim = a.shape
    lb = b.shape[0]
    n = b.shape[1] if nt else b.shape[2]
    b_batched = b_sel is None and lb > 1
    bt = max(ba, lb if b_batched else 1, resid.shape[0] if resid is not None else 1)
    tk = kdim if tk is None else tk
    nk = kdim // tk
    assert m % tm == 0 and n % tn == 0 and kdim % tk == 0, (name, a.shape, b.shape, tm, tn, tk)

    def bidx(count):
        return (lambda z: z) if count > 1 else (lambda z: 0)

    ai = bidx(ba)
    if b_sel is not None:
        bi = lambda z: b_sel
    else:
        bi = bidx(lb)
    if nt:
        b_spec = pl.BlockSpec((None, tn, tk), lambda z, i, j, k: (bi(z), j, k))
        b_block = (tn, tk)
    else:
        b_spec = pl.BlockSpec((None, tk, tn), lambda z, i, j, k: (bi(z), k, j))
        b_block = (tk, tn)
    in_specs = [pl.BlockSpec((None, tm, tk), lambda z, i, j, k: (ai(z), i, k)), b_spec]
    args = [a, b]
    n_b = 1
    epilogue = "plain"
    blocks = [((tm, tk), a.dtype), (b_block, b.dtype)]
    if b2 is not None:
        in_specs.append(b_spec)
        args.append(b2)
        blocks.append((b_block, b2.dtype))
        n_b = 2
        epilogue = "swiglu"
    if resid is not None:
        gi = bidx(gate.shape[0])
        in_specs += [pl.BlockSpec((None, tm, tn), lambda z, i, j, k: (z, i, j)),
                     pl.BlockSpec((None, 1, tn), lambda z, i, j, k: (gi(z), 0, j))]
        args += [resid, gate]
        blocks.append(((tm, tn), resid.dtype))
        epilogue = "resgate"
    out_tn, out_n = tn, n
    if rope is not None:
        head = QK_NOPE + QK_ROPE
        assert tn % (2 * head) == 0 and nk == 1
        out_tn, out_n = tn // head * _HEAD_SLOT, n // head * _HEAD_SLOT
        table_spec = pl.BlockSpec((tm, _HEAD_SLOT), lambda z, i, j, k: (i, 0))
        in_specs += [table_spec, table_spec]
        args += list(rope)
        blocks += [((tm, _HEAD_SLOT), _F32)] * 2
        epilogue = "qrope"
    blocks.append(((tm, out_tn), out_dtype))
    out_specs = pl.BlockSpec((None, tm, out_tn), lambda z, i, j, k: (z, i, j))
    out_shape = jax.ShapeDtypeStruct((bt, m, out_n), out_dtype)
    if side_cast is not None:
        src, sel = side_cast
        steps_i, steps_j = m // tm, n // tn
        br, bc = _side_cast_blocks(src.shape[1], src.shape[2], bt * steps_i * steps_j)
        ncb = src.shape[2] // bc

        def step(z, i, j):
            return (z * steps_i + i) * steps_j + j

        in_specs.append(pl.BlockSpec((None, br, bc), lambda z, i, j, k: (sel, step(z, i, j) // ncb, step(z, i, j) % ncb)))
        args.append(src)
        out_specs = [out_specs, pl.BlockSpec((br, bc), lambda z, i, j, k: (step(z, i, j) // ncb, step(z, i, j) % ncb))]
        out_shape = [out_shape, jax.ShapeDtypeStruct(src.shape[1:], _BF16)]
        blocks += [((br, bc), src.dtype), ((br, bc), _BF16)]
    scratch = [pltpu.VMEM((tm, tn), _F32) for _ in range(n_b)] if nk > 1 else []
    cast_tmp = n_b * _nbytes(b_block, _BF16) if b.dtype != _BF16 else 0
    vmem = (2 * sum(_nbytes(s, dt) for s, dt in blocks) + cast_tmp + (n_b + 2) * _nbytes((tm, tn), _F32)
            + (2 << 20))
    return pl.pallas_call(
        functools.partial(_mm_body, n_b=n_b, epilogue=epilogue, nk=nk, q_scale=q_scale, nt=nt,
                          side=side_cast is not None),
        grid=(bt, m // tm, n // tn, nk),
        in_specs=in_specs,
        out_specs=out_specs,
        out_shape=out_shape,
        scratch_shapes=scratch,
        compiler_params=_params(("parallel", "parallel", "parallel", "arbitrary"), vmem),
        name=name,
    )(*args)


def _dft_cos_sin(n, scale):
    idx = jnp.arange(n, dtype=jnp.int32)
    phase = (idx[:, None] * idx[None, :]) % n
    ang = phase.astype(_F32) * (2.0 * np.pi / n)
    return (jnp.cos(ang) * scale).astype(_BF16), (jnp.sin(ang) * scale).astype(_BF16)


def _dft_gen_body(ca_ref, sa_ref, cb_ref, sb_ref, o_ref, *, n, scale):
    ca, sa, cb, sb = ca_ref[...], sa_ref[...], cb_ref[...], sb_ref[...]
    o_ref[:, :n] = ((ca * cb - sa * sb) * scale).astype(o_ref.dtype)
    o_ref[:, n:] = ((sa * cb + ca * sb) * -scale).astype(o_ref.dtype)


def _position_dft_matrix(n):
    tr = _tile(n, 256, _SUBLANES_BF16)
    t = jnp.arange(n, dtype=jnp.int32)[None, :]
    two_pi_n = 2.0 * np.pi / n
    beta = ((jnp.arange(tr, dtype=jnp.int32)[:, None] * t) % n).astype(_F32) * two_pi_n
    alpha = ((jnp.arange(0, n, tr, dtype=jnp.int32)[:, None] * t) % n).astype(_F32) * two_pi_n
    alpha = alpha.reshape(n // tr, 1, n)
    row_spec = pl.BlockSpec((None, 1, n), lambda i: (i, 0, 0))
    base_spec = pl.BlockSpec((tr, n), lambda i: (0, 0))
    vmem = 4 * _nbytes((tr, n), _F32) + 2 * _nbytes((tr, 2 * n), _BF16) + 4 * _nbytes((tr, n), _F32) + (2 << 20)
    return pl.pallas_call(
        functools.partial(_dft_gen_body, n=n, scale=float(n ** -0.5)),
        grid=(n // tr,),
        in_specs=[row_spec, row_spec, base_spec, base_spec],
        out_specs=pl.BlockSpec((tr, 2 * n), lambda i: (i, 0)),
        out_shape=jax.ShapeDtypeStruct((n, 2 * n), _BF16),
        compiler_params=_params(("parallel",), vmem),
        name="fourier_position_matrix",
    )(jnp.cos(alpha), jnp.sin(alpha), jnp.cos(beta), jnp.sin(beta))


def _channel_dft(h, cs):
    b, n, d = h.shape
    dg = cs.shape[1]
    groups = d // dg
    tm = _tile(n, 1024, _SUBLANES_BF16)
    vmem = 2 * (_nbytes((tm, dg), _BF16) * 2 + _nbytes((dg, dg), _BF16)) + 2 * _nbytes((tm, dg), _F32) + (2 << 20)
    return pl.pallas_call(
        functools.partial(_mm_body, n_b=1, epilogue="plain", nk=1, q_scale=1.0, nt=False, side=False),
        grid=(b, n // tm, groups, 2),
        in_specs=[pl.BlockSpec((None, tm, dg), lambda z, i, g, c: (z, i, g)),
                  pl.BlockSpec((None, dg, dg), lambda z, i, g, c: (c, 0, 0))],
        out_specs=pl.BlockSpec((None, None, tm, dg), lambda z, i, g, c: (z, c, i, g)),
        out_shape=jax.ShapeDtypeStruct((b, 2, n, d), _BF16),
        compiler_params=_params(("parallel",) * 4, vmem),
        name="fourier_channel_dft",
    )(h, cs)


def _fourier_2d(h, cs, side_cast=None):
    b, n, d = h.shape
    z = _channel_dft(h, cs)
    a = _position_dft_matrix(n)[None]
    return _matmul(a, z.reshape(b, 2 * n, d), tm=_tile(n, 1024, _SUBLANES_BF16), tn=_tile(d, 1024, _LANES),
                   tk=_tile(2 * n, 2048, _LANES), out_dtype=_BF16, side_cast=side_cast,
                   name="fourier_position_dft")


def _mla_prep_body(*refs, with_q, with_rope):
    refs = list(refs)
    a_ref = refs.pop(0)
    gq_ref = refs.pop(0) if with_q else None
    gkv_ref = refs.pop(0)
    cos_ref, sin_ref = (refs.pop(0), refs.pop(0)) if with_rope else (None, None)
    cq_ref = refs.pop(0) if with_q else None
    ckv_ref, kr_ref = refs

    def norm(t, g):
        return t * lax.rsqrt(jnp.mean(t * t, axis=-1, keepdims=True) + EPS) * g

    off = 0
    if with_q:
        cq_ref[...] = norm(a_ref[:, :Q_LORA], gq_ref[...]).astype(cq_ref.dtype)
        off = Q_LORA
    ckv_ref[...] = norm(a_ref[:, off:off + KV_LORA], gkv_ref[...]).astype(ckv_ref.dtype)
    kr = a_ref[:, off + KV_LORA:off + KV_LORA + _LANES]
    if with_rope:
        kr = _rope_rotate(kr, cos_ref[:, :_LANES], sin_ref[:, :_LANES], False)
    kr_ref[:, :_LANES] = kr.astype(kr_ref.dtype)
    kr_ref[:, _LANES:] = pltpu.roll(kr, QK_ROPE, 1).astype(kr_ref.dtype)


def _mla_prep(a, g_q, g_kv, rope):
    b, n, w = a.shape
    with_q = g_q is not None
    with_rope = rope is not None
    tr = _tile(n, 256, _SUBLANES_BF16)
    row = lambda z, i: (z, i, 0)
    in_specs = [pl.BlockSpec((None, tr, w), row)]
    args = [a]
    if with_q:
        in_specs.append(pl.BlockSpec((1, Q_LORA), lambda z, i: (0, 0)))
        args.append(g_q.reshape(1, Q_LORA))
    in_specs.append(pl.BlockSpec((1, KV_LORA), lambda z, i: (0, 0)))
    args.append(g_kv.reshape(1, KV_LORA))
    if with_rope:
        in_specs += [pl.BlockSpec((tr, _HEAD_SLOT), lambda z, i: (i, 0))] * 2
        args += list(rope)
    out_specs, out_shape = [], []
    if with_q:
        out_specs.append(pl.BlockSpec((None, tr, Q_LORA), row))
        out_shape.append(jax.ShapeDtypeStruct((b, n, Q_LORA), _BF16))
    out_specs += [pl.BlockSpec((None, tr, KV_LORA), row), pl.BlockSpec((None, tr, _HEAD_SLOT), row)]
    out_shape += [jax.ShapeDtypeStruct((b, n, KV_LORA), _BF16), jax.ShapeDtypeStruct((b, n, _HEAD_SLOT), _BF16)]
    vmem = 6 * _nbytes((tr, w), _F32) + (2 << 20)
    return pl.pallas_call(
        functools.partial(_mla_prep_body, with_q=with_q, with_rope=with_rope),
        grid=(b, n // tr),
        in_specs=in_specs,
        out_specs=out_specs,
        out_shape=out_shape,
        compiler_params=_params(("parallel", "parallel"), vmem),
        name="mla_prep_latent" if with_q else "mla_prep_context",
    )(*args)


def _attn_body(q_ref, k_ref, kr_ref, vt_ref, o_ref, kcat_ref, vext_ref, *, chunks):
    @pl.when(pl.program_id(2) == 0)
    def _():
        kcat_ref[:, :QK_NOPE] = k_ref[...]
        kcat_ref[:, QK_NOPE:] = kr_ref[...]
        vext_ref[:V_HEAD, :] = vt_ref[...]
        vext_ref[V_HEAD:, :] = jnp.ones((_SUBLANES_BF16, vext_ref.shape[1]), vext_ref.dtype)

    q = q_ref[...]

    def scores(start, size):
        return lax.dot_general(kcat_ref[start:start + size, :], q, (((1,), (1,)), ((), ())),
                               preferred_element_type=_F32)

    starts = [sum(chunks[:c]) for c in range(len(chunks))]
    m = acc = None
    ahead = [scores(starts[c], chunks[c]) for c in range(min(_ATTN_LOOKAHEAD, len(chunks)))]
    for c, (start, size) in enumerate(zip(starts, chunks)):
        st = ahead.pop(0)
        nxt = c + _ATTN_LOOKAHEAD
        if nxt < len(chunks):
            ahead.append(scores(starts[nxt], chunks[nxt]))
        mc = jnp.max(st, axis=0, keepdims=True).astype(_BF16)
        m_new = mc if m is None else jnp.maximum(m, mc)
        p = jnp.exp2(st.astype(_BF16) - m_new)
        pv = jnp.dot(vext_ref[:, start:start + size], p, preferred_element_type=_F32)
        if m is None:
            acc = pv
        else:
            acc = jnp.exp2(m.astype(_F32) - m_new.astype(_F32)) * acc + pv
        m = m_new
    o_ref[...] = (acc[:V_HEAD] / acc[V_HEAD:V_HEAD + 1]).T.astype(o_ref.dtype)


def _attention(q, k, kr, vt):
    b, n, _ = q.shape
    nk = k.shape[1]
    tq = _tile(n, 1024, _LANES)
    chunk = _ATTN_KEY_CHUNK
    chunks = [chunk] * (nk // chunk) + ([nk % chunk] if nk % chunk else [])
    vmem = (4 * _nbytes((tq, _HEAD_SLOT), _BF16) + 5 * _nbytes((nk, _LANES), _BF16)
            + 2 * _nbytes((V_HEAD, nk), _BF16) + 6 * _nbytes((min(chunk, nk), tq), _F32) + (4 << 20))
    return pl.pallas_call(
        functools.partial(_attn_body, chunks=chunks),
        grid=(b, MLA_HEADS, n // tq),
        in_specs=[pl.BlockSpec((None, tq, _HEAD_SLOT), lambda z, h, i: (z, i, h)),
                  pl.BlockSpec((None, nk, QK_NOPE), lambda z, h, i: (z, 0, h)),
                  pl.BlockSpec((None, nk, _LANES), lambda z, h, i: (z, 0, h % 2)),
                  pl.BlockSpec((None, V_HEAD, nk), lambda z, h, i: (z, h, 0))],
        out_specs=pl.BlockSpec((None, tq, V_HEAD), lambda z, h, i: (z, i, h)),
        out_shape=jax.ShapeDtypeStruct((b, n, MLA_HEADS * V_HEAD), _BF16),
        scratch_shapes=[pltpu.VMEM((nk, _HEAD_SLOT), _BF16), pltpu.VMEM((V_HEAD + _SUBLANES_BF16, nk), _BF16)],
        compiler_params=_params(("parallel", "parallel", "arbitrary"), vmem),
        name="mla_attention",
    )(q, k, kr, vt)


def _rope_tables(n):
    rows_n = n // GRID_W
    row = jnp.repeat(jnp.arange(rows_n, dtype=_F32), GRID_W)
    col = jnp.tile(jnp.arange(GRID_W, dtype=_F32), rows_n)
    n_freq = QK_ROPE // 4
    inv_freq = ROPE_BASE ** (-jnp.arange(n_freq, dtype=_F32) / n_freq)
    ar, ac = row[:, None] * inv_freq, col[:, None] * inv_freq
    zeros = jnp.zeros((n, _LANES - QK_ROPE), _F32)
    cos = jnp.concatenate([jnp.cos(ar), jnp.cos(ar), jnp.cos(ac), jnp.cos(ac)], axis=1)
    sin = jnp.concatenate([-jnp.sin(ar), jnp.sin(ar), -jnp.sin(ac), jnp.sin(ac)], axis=1)
    return (jnp.concatenate([cos, zeros, zeros, cos], axis=1), jnp.concatenate([sin, zeros, zeros, sin], axis=1))


def _mla_mix(h, hc, w_a, g_q, g_kv, w_uq, w_ukv, w_o, w_o_sel, x, gate):
    b, n, d = h.shape
    nc = hc.shape[1]
    rope = _rope_tables(n)
    a_width = Q_LORA + KV_LORA + QK_ROPE
    a_pad = -(-a_width // _HEAD_SLOT) * _HEAD_SLOT
    w_a_pad = jnp.pad(w_a, ((0, 0), (0, a_pad - a_width))).astype(_BF16)[None]
    tm = _tile(n, 1024, _SUBLANES_BF16)
    a_l = _matmul(h, w_a_pad, tm=tm, tn=_HEAD_SLOT, out_dtype=_F32, name="mla_down_latent")
    a_c = _matmul(hc, w_a_pad[:, :, Q_LORA:], tm=_tile(nc, 1024, _SUBLANES_BF16), tn=_HEAD_SLOT,
                  out_dtype=_F32, name="mla_down_context")
    cq, ckv_l, kr_l = _mla_prep(a_l, g_q, g_kv, rope)
    ckv_c, kr_c = _mla_prep(a_c, None, g_kv, None)
    ckv = jnp.concatenate([ckv_c, ckv_l], axis=1)
    kr = jnp.concatenate([kr_c, kr_l], axis=1)

    head = QK_NOPE + QK_ROPE
    q, w_o_bf16 = _matmul(cq, w_uq[None], tm=tm, tn=_tile(MLA_HEADS * head, 4 * head, 2 * head), out_dtype=_BF16,
                          rope=rope, q_scale=float(head ** -0.5 * _LOG2E), side_cast=(w_o, w_o_sel),
                          name="mla_q_up")
    nkv = nc + n
    w_ukv_h = w_ukv.reshape(KV_LORA, MLA_HEADS // 2, 2, QK_NOPE + V_HEAD)
    w_uk_h = w_ukv_h[..., :QK_NOPE]
    w_uk_h = jnp.stack([w_uk_h[:, :, 0], jnp.roll(w_uk_h[:, :, 1], -(QK_NOPE // 2), axis=-1)], axis=2)
    w_uk = w_uk_h.reshape(KV_LORA, MLA_HEADS * QK_NOPE).astype(_BF16)[None]
    w_ukv_h = w_ukv_h.reshape(KV_LORA, MLA_HEADS, QK_NOPE + V_HEAD)
    w_uv_t = w_ukv_h[:, :, QK_NOPE:].reshape(KV_LORA, MLA_HEADS * V_HEAD).T.astype(_BF16)[None]
    k = _matmul(ckv, w_uk, tm=_tile(nkv, 1088, _SUBLANES_BF16), tn=_tile(MLA_HEADS * QK_NOPE, 2048, _LANES),
                out_dtype=_BF16, name="mla_k_up")
    vt = _matmul(w_uv_t, ckv, nt=True, tm=_tile(MLA_HEADS * V_HEAD, 1024, _SUBLANES_BF16),
                 tn=_tile(nkv, 2176, _LANES), out_dtype=_BF16, name="mla_v_up_transposed")
    o = _attention(q, k, kr, vt)
    return _matmul(o, w_o_bf16[None], tm=_tile(n, 512, _SUBLANES_BF16), tn=_tile(d, 512, _LANES),
                   out_dtype=_F32, resid=x, gate=gate, name="mla_out_proj")


def _ffn(x, h, w_gate, w_up, w_down, layer, gate, w_down_bf16=None):
    b, n, d = x.shape
    hidden = w_gate.shape[2]
    tm = _tile(n, 1024, _SUBLANES_BF16)
    u = _matmul(h, w_gate, b2=w_up, b_sel=layer, tm=tm, tn=_tile(hidden, 256, _LANES), out_dtype=_BF16,
                side_cast=None if w_down_bf16 is not None else (w_down, layer), name="ffn_gate_up")
    if w_down_bf16 is None:
        u, w_down_bf16 = u
    half = hidden // 2
    tk = half if half % _LANES == 0 else hidden
    out = _matmul(u, w_down_bf16[None], tm=tm, tn=_tile(d, 512, _LANES), tk=tk, out_dtype=_F32, resid=x,
                  gate=gate, name="ffn_down")
    return out, w_down_bf16


def kernel(x, c, ctx, c_ctx, w_ada, b_ada, g_mix, g_ffn, fourier_w_out, mla_w_a, mla_g_q, mla_g_kv, mla_w_uq,
           mla_w_ukv, mla_w_o, w_gate, w_up, w_down, g_final):
    b, n, d = x.shape
    depth = w_ada.shape[0]
    cond = jnp.concatenate([c, c_ctx[None]], axis=0)
    cond = jnp.pad(cond, ((0, -(b + 1) % 8), (0, 0)))
    mod = _ada_modulation(cond, w_ada, b_ada)
    dg = d // FOURIER_GROUPS
    cs = jnp.stack(_dft_cos_sin(dg, dg ** -0.5))

    xc = ctx
    for i in range(depth):
        need_ctx = i < depth - 1
        j = i // 2
        lat = [mod[i, :b, s * d:(s + 1) * d].reshape(b, 1, d) for s in range(N_MOD)]
        con = [mod[i, b:b + 1, s * d:(s + 1) * d].reshape(1, 1, d) for s in range(N_MOD)]
        h = _rms_norm(x, g_mix[i], lat[1], lat[0], out_dtype=_BF16)
        hc = _rms_norm(xc, g_mix[i], con[1], con[0], out_dtype=_BF16)
        tm = _tile(n, 1024, _SUBLANES_BF16)
        if i % 2 == 0:
            f, w_out = _fourier_2d(h, cs, side_cast=(fourier_w_out, j))
            w_out = w_out[None]
            x = _matmul(f, w_out, tm=tm, tn=_tile(d, 512, _LANES), out_dtype=_F32, resid=x, gate=lat[2],
                        name="fourier_out_proj")
            if need_ctx:
                xc = _matmul(_fourier_2d(hc, cs), w_out, tm=_tile(xc.shape[1], 1024, _SUBLANES_BF16),
                             tn=_tile(d, 512, _LANES), out_dtype=_F32, resid=xc, gate=con[2],
                             name="fourier_out_proj_ctx")
        else:
            assert not need_ctx, "context queries are only implemented for the Fourier mixer layers"
            x = _mla_mix(h, hc, mla_w_a[j], mla_g_q[j], mla_g_kv[j], mla_w_uq[j], mla_w_ukv[j], mla_w_o, j,
                         x, lat[2])
        x, w_down_i = _ffn(x, _rms_norm(x, g_ffn[i], lat[4], lat[3], out_dtype=_BF16), w_gate, w_up, w_down, i,
                           lat[5])
        if need_ctx:
            xc, _ = _ffn(xc, _rms_norm(xc, g_ffn[i], con[4], con[3], out_dtype=_BF16), w_gate, w_up, w_down, i,
                         con[5], w_down_bf16=w_down_i)
    return _rms_norm(x, g_final, out_dtype=x.dtype)
```

```python
import functools
import math

import jax
import jax.numpy as jnp
import numpy as np
from jax import lax
from jax.experimental import pallas as pl
from jax.experimental.pallas import tpu as pltpu

_F32 = jnp.float32
_BF16 = jnp.bfloat16

GRID_W = 64
FOURIER_GROUPS = 4
MLA_HEADS = 64
Q_LORA = 1536
KV_LORA = 512
QK_NOPE = 128
QK_ROPE = 64
V_HEAD = 128
ROPE_BASE = 10000.0
EPS = 1e-6
N_MOD = 6

_LANES = 128
_SUBLANES_BF16 = 16
_VMEM_CAP_BYTES = 56 * 1024 * 1024
_HEAD_SLOT = 2 * _LANES
_ATTN_KEY_CHUNK = 512
_ATTN_LOOKAHEAD = 2
_LOG2E = math.log2(math.e)


def _tile(dim, pref, align):
    if dim <= pref:
        return dim
    t = (pref // align) * align
    while t >= align:
        if dim % t == 0:
            return t
        t -= align
    return dim


def _nbytes(shape, dtype):
    return int(np.prod(shape)) * jnp.dtype(dtype).itemsize


def _params(semantics, vmem_bytes):
    return pltpu.CompilerParams(dimension_semantics=semantics,
                                vmem_limit_bytes=int(min(_VMEM_CAP_BYTES, vmem_bytes)))


def _ada_body(c_ref, w_ref, b_ref, o_ref):
    c = c_ref[...]
    s = c * jax.nn.sigmoid(c)
    acc = jnp.dot(s.astype(_BF16), w_ref[...].astype(_BF16), preferred_element_type=_F32)
    o_ref[...] = acc + b_ref[...]


def _ada_modulation(cc, w_ada, b_ada):
    depth, d, n = w_ada.shape
    r = cc.shape[0]
    tn = _tile(n, 512, _LANES)
    vmem = 2 * _nbytes((d, tn), _F32) + _nbytes((d, tn), _BF16) + 8 * _nbytes((r, tn), _F32) + (4 << 20)
    return pl.pallas_call(
        _ada_body,
        grid=(depth, n // tn),
        in_specs=[
            pl.BlockSpec((r, d), lambda l, j: (0, 0)),
            pl.BlockSpec((None, d, tn), lambda l, j: (l, 0, j)),
            pl.BlockSpec((None, 1, tn), lambda l, j: (l, 0, j)),
        ],
        out_specs=pl.BlockSpec((None, r, tn), lambda l, j: (l, 0, j)),
        out_shape=jax.ShapeDtypeStruct((depth, r, n), _F32),
        compiler_params=_params(("parallel", "parallel"), vmem),
        name="ada_modulation",
    )(cc, w_ada, b_ada.reshape(depth, 1, n))


def _norm_body(*refs, modulate):
    if modulate:
        x_ref, g_ref, sc_ref, sh_ref, o_ref = refs
    else:
        x_ref, g_ref, o_ref = refs
    x = x_ref[...]
    y = x * lax.rsqrt(jnp.mean(x * x, axis=-1, keepdims=True) + EPS) * g_ref[...]
    if modulate:
        y = y * (1.0 + sc_ref[...]) + sh_ref[...]
    o_ref[...] = y.astype(o_ref.dtype)


def _rms_norm(x, g, scale=None, shift=None, *, out_dtype):
    b, n, d = x.shape
    tr = _tile(n, 256, 8)
    modulate = scale is not None
    in_specs = [pl.BlockSpec((None, tr, d), lambda bb, i: (bb, i, 0)),
                pl.BlockSpec((1, d), lambda bb, i: (0, 0))]
    args = [x, g.reshape(1, d)]
    if modulate:
        per_batch = scale.shape[0] > 1
        mod_spec = pl.BlockSpec((None, 1, d), (lambda bb, i: (bb, 0, 0)) if per_batch else (lambda bb, i: (0, 0, 0)))
        in_specs += [mod_spec, mod_spec]
        args += [scale, shift]
    vmem = 2 * _nbytes((tr, d), _F32) + 2 * _nbytes((tr, d), out_dtype) + 3 * _nbytes((tr, d), _F32) + (2 << 20)
    return pl.pallas_call(
        functools.partial(_norm_body, modulate=modulate),
        grid=(b, n // tr),
        in_specs=in_specs,
        out_specs=pl.BlockSpec((None, tr, d), lambda bb, i: (bb, i, 0)),
        out_shape=jax.ShapeDtypeStruct((b, n, d), out_dtype),
        compiler_params=_params(("parallel", "parallel"), vmem),
        name="rms_norm_modulate" if modulate else "rms_norm",
    )(*args)


def _rope_rotate(t, cos, sin, upper_half):
    lane = lax.broadcasted_iota(jnp.int32, t.shape, 1)
    quarter = QK_ROPE // 4
    even_chunk = (lane // quarter) % 2 == 0
    swapped = jnp.where(even_chunk, pltpu.roll(t, _LANES - quarter, 1), pltpu.roll(t, quarter, 1))
    valid = (lane >= QK_ROPE) if upper_half else (lane < QK_ROPE)
    return jnp.where(valid, t * cos + swapped * sin, 0.0)


def _q_head_slots(acc, cos, sin, q_scale):
    tiles = [acc[:, i * _LANES:(i + 1) * _LANES] for i in range(acc.shape[1] // _LANES)]
    low_half = lax.broadcasted_iota(jnp.int32, tiles[0].shape, 1) < QK_ROPE
    cos_lo, cos_hi = cos[:, :_LANES], cos[:, _LANES:]
    sin_lo, sin_hi = sin[:, :_LANES], sin[:, _LANES:]
    parts = []
    for p in range(len(tiles) // 3):
        t0, t1, t2 = tiles[3 * p:3 * p + 3]
        parts += [t0 * q_scale, _rope_rotate(t1, cos_lo, sin_lo, False) * q_scale,
                  jnp.where(low_half, t2, t1) * q_scale, _rope_rotate(t2, cos_hi, sin_hi, True) * q_scale]
    return jnp.concatenate(parts, axis=1)


def _side_cast_blocks(rows, cols, steps):
    widths = [c for c in range(_LANES, cols + 1, _LANES) if cols % c == 0]
    for bc in sorted(widths, key=lambda c: abs(c - 4 * _LANES)):
        ncb = cols // bc
        if steps % ncb == 0 and rows % (steps // ncb) == 0 and (rows // (steps // ncb)) % _SUBLANES_BF16 == 0:
            return rows // (steps // ncb), bc
    raise ValueError(f"no tile-aligned split of {(rows, cols)} into {steps} blocks")


def _mm_body(*refs, n_b, epilogue, nk, q_scale, nt, side):
    a_ref = refs[0]
    b_refs = refs[1:1 + n_b]
    pos = 1 + n_b
    extra = ()
    if epilogue in ("resgate", "qrope"):
        extra = refs[pos:pos + 2]
        pos += 2
    elif epilogue == "mirror":
        extra = refs[pos:pos + 1]
        pos += 1
    if side:
        refs[pos + 2][...] = refs[pos][...].astype(refs[pos + 2].dtype)
        refs = refs[:pos] + refs[pos + 1:pos + 2] + refs[pos + 3:]
    o_ref = refs[pos]
    acc_refs = refs[pos + 1:]

    def finish(accs):
        if epilogue == "plain":
            out = accs[0]
        elif epilogue == "swiglu":
            g, u = accs
            out = g * jax.nn.sigmoid(g) * u
        elif epilogue == "resgate":
            res_ref, gate_ref = extra
            out = res_ref[...] + gate_ref[...] * accs[0]
        elif epilogue == "qrope":
            cos_ref, sin_ref = extra
            out = _q_head_slots(accs[0], cos_ref[...], sin_ref[...], q_scale)
        elif epilogue == "mirror":
            p, neg_q = accs
            plus = p - neg_q
            first = plus[:, :_LANES]
            lane = lax.broadcasted_iota(jnp.int32, first.shape, 1)
            first = jnp.where(lane == 0, extra[0][...].astype(_F32), first)
            out = jnp.concatenate([p + neg_q, first, plus[:, _LANES:]], axis=1)
        o_ref[...] = out.astype(o_ref.dtype)

    def dots():
        a = a_ref[...]
        outs = []
        for b_ref in b_refs:
            b = b_ref[...].astype(_BF16)
            if nt:
                outs.append(lax.dot_general(a, b, (((1,), (1,)), ((), ())), preferred_element_type=_F32))
            else:
                outs.append(jnp.dot(a, b, preferred_element_type=_F32))
        return outs

    if nk == 1:
        finish(dots())
        return

    k = pl.program_id(3)

    if epilogue == "mirror":
        half = nk // 2
        for idx, first_k in ((0, 0), (1, half)):
            @pl.when(k == first_k)
            def _(idx=idx):
                acc_refs[idx][...] = dots()[0]

            @pl.when((k > first_k) & (k < first_k + half))
            def _(idx=idx):
                acc_refs[idx][...] += dots()[0]
    else:
        @pl.when(k == 0)
        def _():
            for acc_ref, d in zip(acc_refs, dots()):
                acc_ref[...] = d

        @pl.when(k > 0)
        def _():
            for acc_ref, d in zip(acc_refs, dots()):
                acc_ref[...] += d

    @pl.when(k == nk - 1)
    def _():
        finish([acc_ref[...] for acc_ref in acc_refs])


def _matmul(a, b, *, tm, tn, tk=None, out_dtype, b2=None, b_sel=None, nt=False, resid=None, gate=None,
            rope=None, q_scale=1.0, side_cast=None, mirror=None, k_extent=None, name):
    ba, m, kdim = a.shape
    kdim = kdim if k_extent is None else k_extent
    lb = b.shape[0]
    n = b.shape[1] if nt else b.shape[2]
    b_batched = b_sel is None and lb > 1
    bt = max(ba, lb if b_batched else 1, resid.shape[0] if resid is not None else 1)
    tk = kdim if tk is None else tk
    nk = kdim // tk
    assert m % tm == 0 and n % tn == 0 and kdim % tk == 0, (name, a.shape, b.shape, tm, tn, tk)

    def bidx(count):
        return (lambda z: z) if count > 1 else (lambda z: 0)

    ai = bidx(ba)
    if b_sel is not None:
        bi = lambda z: b_sel
    else:
        bi = bidx(lb)
    if nt:
        b_spec = pl.BlockSpec((None, tn, tk), lambda z, i, j, k: (bi(z), j, k))
        b_block = (tn, tk)
    else:
        b_spec = pl.BlockSpec((None, tk, tn), lambda z, i, j, k: (bi(z), k, j))
        b_block = (tk, tn)
    in_specs = [pl.BlockSpec((None, tm, tk), lambda z, i, j, k: (ai(z), i, k)), b_spec]
    args = [a, b]
    n_b = 1
    epilogue = "plain"
    blocks = [((tm, tk), a.dtype), (b_block, b.dtype)]
    if b2 is not None:
        in_specs.append(b_spec)
        args.append(b2)
        blocks.append((b_block, b2.dtype))
        n_b = 2
        epilogue = "swiglu"
    if resid is not None:
        gi = bidx(gate.shape[0])
        in_specs += [pl.BlockSpec((None, tm, tn), lambda z, i, j, k: (z, i, j)),
                     pl.BlockSpec((None, 1, tn), lambda z, i, j, k: (gi(z), 0, j))]
        args += [resid, gate]
        blocks.append(((tm, tn), resid.dtype))
        epilogue = "resgate"
    out_tn, out_n = tn, n
    if rope is not None:
        head = QK_NOPE + QK_ROPE
        assert tn % (2 * head) == 0 and nk == 1
        out_tn, out_n = tn // head * _HEAD_SLOT, n // head * _HEAD_SLOT
        table_spec = pl.BlockSpec((tm, _HEAD_SLOT), lambda z, i, j, k: (i, 0))
        in_specs += [table_spec, table_spec]
        args += list(rope)
        blocks += [((tm, _HEAD_SLOT), _F32)] * 2
        epilogue = "qrope"
    n_acc = n_b
    if mirror is not None:
        assert nk % 2 == 0 and tn % _LANES == 0
        out_tn, out_n, n_acc = 2 * tn, 2 * n, 2
        in_specs.append(pl.BlockSpec((None, tm, _LANES), lambda z, i, j, k: (z, i, j)))
        args.append(mirror)
        blocks.append(((tm, _LANES), mirror.dtype))
        epilogue = "mirror"
    blocks.append(((tm, out_tn), out_dtype))
    out_specs = pl.BlockSpec((None, tm, out_tn), lambda z, i, j, k: (z, i, j))
    out_shape = jax.ShapeDtypeStruct((bt, m, out_n), out_dtype)
    if side_cast is not None:
        src, sel = side_cast
        steps_i, steps_j = m // tm, n // tn
        br, bc = _side_cast_blocks(src.shape[1], src.shape[2], bt * steps_i * steps_j)
        ncb = src.shape[2] // bc

        def step(z, i, j):
            return (z * steps_i + i) * steps_j + j

        in_specs.append(pl.BlockSpec((None, br, bc), lambda z, i, j, k: (sel, step(z, i, j) // ncb, step(z, i, j) % ncb)))
        args.append(src)
        out_specs = [out_specs, pl.BlockSpec((br, bc), lambda z, i, j, k: (step(z, i, j) // ncb, step(z, i, j) % ncb))]
        out_shape = [out_shape, jax.ShapeDtypeStruct(src.shape[1:], _BF16)]
        blocks += [((br, bc), src.dtype), ((br, bc), _BF16)]
    scratch = [pltpu.VMEM((tm, tn), _F32) for _ in range(n_acc)] if nk > 1 else []
    cast_tmp = n_b * _nbytes(b_block, _BF16) if b.dtype != _BF16 else 0
    vmem = (2 * sum(_nbytes(s, dt) for s, dt in blocks) + cast_tmp + (n_b + 2) * _nbytes((tm, tn), _F32)
            + (2 << 20))
    return pl.pallas_call(
        functools.partial(_mm_body, n_b=n_b, epilogue=epilogue, nk=nk, q_scale=q_scale, nt=nt,
                          side=side_cast is not None),
        grid=(bt, m // tm, n // tn, nk),
        in_specs=in_specs,
        out_specs=out_specs,
        out_shape=out_shape,
        scratch_shapes=scratch,
        compiler_params=_params(("parallel", "parallel", "parallel", "arbitrary"), vmem),
        name=name,
    )(*args)


def _dft_cos_sin(n, scale):
    idx = jnp.arange(n, dtype=jnp.int32)
    phase = (idx[:, None] * idx[None, :]) % n
    ang = phase.astype(_F32) * (2.0 * np.pi / n)
    return (jnp.cos(ang) * scale).astype(_BF16), (jnp.sin(ang) * scale).astype(_BF16)


def _dft_gen_body(ca_ref, sa_ref, cb_ref, sb_ref, o_ref, *, n, scale):
    ca, sa, cb, sb = ca_ref[...], sa_ref[...], cb_ref[...], sb_ref[...]
    o_ref[:, :n] = ((ca * cb - sa * sb) * scale).astype(o_ref.dtype)
    o_ref[:, n:] = ((sa * cb + ca * sb) * -scale).astype(o_ref.dtype)


def _position_dft_matrix(n):
    tr = _tile(n, 256, _SUBLANES_BF16)
    t = jnp.arange(n, dtype=jnp.int32)[None, :]
    two_pi_n = 2.0 * np.pi / n
    beta = ((jnp.arange(tr, dtype=jnp.int32)[:, None] * t) % n).astype(_F32) * two_pi_n
    alpha = ((jnp.arange(0, n, tr, dtype=jnp.int32)[:, None] * t) % n).astype(_F32) * two_pi_n
    alpha = alpha.reshape(n // tr, 1, n)
    row_spec = pl.BlockSpec((None, 1, n), lambda i: (i, 0, 0))
    base_spec = pl.BlockSpec((tr, n), lambda i: (0, 0))
    vmem = 4 * _nbytes((tr, n), _F32) + 2 * _nbytes((tr, 2 * n), _BF16) + 4 * _nbytes((tr, n), _F32) + (2 << 20)
    return pl.pallas_call(
        functools.partial(_dft_gen_body, n=n, scale=float(n ** -0.5)),
        grid=(n // tr,),
        in_specs=[row_spec, row_spec, base_spec, base_spec],
        out_specs=pl.BlockSpec((tr, 2 * n), lambda i: (i, 0)),
        out_shape=jax.ShapeDtypeStruct((n, 2 * n), _BF16),
        compiler_params=_params(("parallel",), vmem),
        name="fourier_position_matrix",
    )(jnp.cos(alpha), jnp.sin(alpha), jnp.cos(beta), jnp.sin(beta))


def _channel_dft_body(h_ref, cs_ref, mid_w_ref, z_ref, mid_ref):
    h = h_ref[...]
    z_ref[...] = jnp.dot(h, cs_ref[...], preferred_element_type=_F32).astype(z_ref.dtype)

    @pl.when(pl.program_id(3) == 0)
    def _():
        mid_ref[...] = jnp.dot(h, mid_w_ref[...], preferred_element_type=_F32).astype(mid_ref.dtype)


def _channel_dft_half(h, cs_half, mid_w):
    b, n, d = h.shape
    dg, dh = cs_half.shape[1:]
    groups = d // dg
    tm = _tile(n, 1024, _SUBLANES_BF16)
    vmem = (2 * (_nbytes((tm, dg), _BF16) + _nbytes((dg, dh), _BF16) + _nbytes((dg, _LANES), _BF16)
                 + _nbytes((tm, dh), _BF16) + _nbytes((tm, _LANES), _BF16)) + 3 * _nbytes((tm, dh), _F32) + (2 << 20))
    return pl.pallas_call(
        _channel_dft_body,
        grid=(b, n // tm, groups, 2),
        in_specs=[pl.BlockSpec((None, tm, dg), lambda z, i, g, c: (z, i, g)),
                  pl.BlockSpec((None, dg, dh), lambda z, i, g, c: (c, 0, 0)),
                  pl.BlockSpec((dg, _LANES), lambda z, i, g, c: (0, 0))],
        out_specs=[pl.BlockSpec((None, None, tm, dh), lambda z, i, g, c: (z, c, i, g)),
                   pl.BlockSpec((None, tm, _LANES), lambda z, i, g, c: (z, i, g))],
        out_shape=[jax.ShapeDtypeStruct((b, 2, n, groups * dh), _BF16),
                   jax.ShapeDtypeStruct((b, n, groups * _LANES), _BF16)],
        compiler_params=_params(("parallel", "parallel", "parallel", "arbitrary"), vmem),
        name="fourier_channel_dft",
    )(h, cs_half, mid_w)


def _fourier_tables(dg):
    cc, sc = _dft_cos_sin(dg, dg ** -0.5)
    half = dg // 2
    mid_w = jnp.pad(cc[:, half:half + 1], ((0, 0), (0, _LANES - 1)))
    return jnp.stack([cc[:, :half], sc[:, :half]]), mid_w


def _mirror_row_order(d, dg):
    half = dg // 2
    group = np.concatenate([np.arange(half), [half], np.arange(dg - 1, half, -1)])
    return np.concatenate([g * dg + group for g in range(d // dg)])


def _fourier_2d(h, tables):
    b, n, d = h.shape
    cs_half, mid_w = tables
    z, mid = _channel_dft_half(h, cs_half, mid_w)
    a = _position_dft_matrix(n)[None]
    tm = _tile(n, 1024, _SUBLANES_BF16)
    e = _matmul(a, mid, tm=tm, tn=mid.shape[2], tk=_tile(n, 2048, _LANES), k_extent=n, out_dtype=_BF16,
                name="fourier_position_dft_mid")
    return _matmul(a, z.reshape(b, 2 * n, d // 2), tm=tm, tn=cs_half.shape[2], tk=_tile(n, 2048, _LANES),
                   out_dtype=_BF16, mirror=e, name="fourier_position_dft")


def _mla_prep_body(*refs, with_q, with_rope):
    refs = list(refs)
    a_ref = refs.pop(0)
    gq_ref = refs.pop(0) if with_q else None
    gkv_ref = refs.pop(0)
    cos_ref, sin_ref = (refs.pop(0), refs.pop(0)) if with_rope else (None, None)
    cq_ref = refs.pop(0) if with_q else None
    ckv_ref, kr_ref = refs

    def norm(t, g):
        return t * lax.rsqrt(jnp.mean(t * t, axis=-1, keepdims=True) + EPS) * g

    off = 0
    if with_q:
        cq_ref[...] = norm(a_ref[:, :Q_LORA], gq_ref[...]).astype(cq_ref.dtype)
        off = Q_LORA
    ckv_ref[...] = norm(a_ref[:, off:off + KV_LORA], gkv_ref[...]).astype(ckv_ref.dtype)
    kr = a_ref[:, off + KV_LORA:off + KV_LORA + _LANES]
    if with_rope:
        kr = _rope_rotate(kr, cos_ref[:, :_LANES], sin_ref[:, :_LANES], False)
    kr_ref[:, :_LANES] = kr.astype(kr_ref.dtype)
    kr_ref[:, _LANES:] = pltpu.roll(kr, QK_ROPE, 1).astype(kr_ref.dtype)


def _mla_prep(a, g_q, g_kv, rope):
    b, n, w = a.shape
    with_q = g_q is not None
    with_rope = rope is not None
    tr = _tile(n, 256, _SUBLANES_BF16)
    row = lambda z, i: (z, i, 0)
    in_specs = [pl.BlockSpec((None, tr, w), row)]
    args = [a]
    if with_q:
        in_specs.append(pl.BlockSpec((1, Q_LORA), lambda z, i: (0, 0)))
        args.append(g_q.reshape(1, Q_LORA))
    in_specs.append(pl.BlockSpec((1, KV_LORA), lambda z, i: (0, 0)))
    args.append(g_kv.reshape(1, KV_LORA))
    if with_rope:
        in_specs += [pl.BlockSpec((tr, _HEAD_SLOT), lambda z, i: (i, 0))] * 2
        args += list(rope)
    out_specs, out_shape = [], []
    if with_q:
        out_specs.append(pl.BlockSpec((None, tr, Q_LORA), row))
        out_shape.append(jax.ShapeDtypeStruct((b, n, Q_LORA), _BF16))
    out_specs += [pl.BlockSpec((None, tr, KV_LORA), row), pl.BlockSpec((None, tr, _HEAD_SLOT), row)]
    out_shape += [jax.ShapeDtypeStruct((b, n, KV_LORA), _BF16), jax.ShapeDtypeStruct((b, n, _HEAD_SLOT), _BF16)]
    vmem = 6 * _nbytes((tr, w), _F32) + (2 << 20)
    return pl.pallas_call(
        functools.partial(_mla_prep_body, with_q=with_q, with_rope=with_rope),
        grid=(b, n // tr),
        in_specs=in_specs,
        out_specs=out_specs,
        out_shape=out_shape,
        compiler_params=_params(("parallel", "parallel"), vmem),
        name="mla_prep_latent" if with_q else "mla_prep_context",
    )(*args)


def _attn_body(q_ref, k_ref, kr_ref, vt_ref, o_ref, kcat_ref, vext_ref, *, chunks):
    @pl.when(pl.program_id(2) == 0)
    def _():
        kcat_ref[:, :QK_NOPE] = k_ref[...]
        kcat_ref[:, QK_NOPE:] = kr_ref[...]
        vext_ref[:V_HEAD, :] = vt_ref[...]
        vext_ref[V_HEAD:, :] = jnp.ones((_SUBLANES_BF16, vext_ref.shape[1]), vext_ref.dtype)

    q = q_ref[...]

    def scores(start, size):
        return lax.dot_general(kcat_ref[start:start + size, :], q, (((1,), (1,)), ((), ())),
                               preferred_element_type=_F32)

    starts = [sum(chunks[:c]) for c in range(len(chunks))]
    m = acc = None
    ahead = [scores(starts[c], chunks[c]) for c in range(min(_ATTN_LOOKAHEAD, len(chunks)))]
    for c, (start, size) in enumerate(zip(starts, chunks)):
        st = ahead.pop(0)
        nxt = c + _ATTN_LOOKAHEAD
        if nxt < len(chunks):
            ahead.append(scores(starts[nxt], chunks[nxt]))
        mc = jnp.max(st, axis=0, keepdims=True).astype(_BF16)
        m_new = mc if m is None else jnp.maximum(m, mc)
        p = jnp.exp2(st.astype(_BF16) - m_new)
        pv = jnp.dot(vext_ref[:, start:start + size], p, preferred_element_type=_F32)
        if m is None:
            acc = pv
        else:
            acc = jnp.exp2(m.astype(_F32) - m_new.astype(_F32)) * acc + pv
        m = m_new
    o_ref[...] = (acc[:V_HEAD] / acc[V_HEAD:V_HEAD + 1]).T.astype(o_ref.dtype)


def _attention(q, k, kr, vt):
    b, n, _ = q.shape
    nk = k.shape[1]
    tq = _tile(n, 1024, _LANES)
    chunk = _ATTN_KEY_CHUNK
    chunks = [chunk] * (nk // chunk) + ([nk % chunk] if nk % chunk else [])
    vmem = (4 * _nbytes((tq, _HEAD_SLOT), _BF16) + 5 * _nbytes((nk, _LANES), _BF16)
            + 2 * _nbytes((V_HEAD, nk), _BF16) + 6 * _nbytes((min(chunk, nk), tq), _F32) + (4 << 20))
    return pl.pallas_call(
        functools.partial(_attn_body, chunks=chunks),
        grid=(b, MLA_HEADS, n // tq),
        in_specs=[pl.BlockSpec((None, tq, _HEAD_SLOT), lambda z, h, i: (z, i, h)),
                  pl.BlockSpec((None, nk, QK_NOPE), lambda z, h, i: (z, 0, h)),
                  pl.BlockSpec((None, nk, _LANES), lambda z, h, i: (z, 0, h % 2)),
                  pl.BlockSpec((None, V_HEAD, nk), lambda z, h, i: (z, h, 0))],
        out_specs=pl.BlockSpec((None, tq, V_HEAD), lambda z, h, i: (z, i, h)),
        out_shape=jax.ShapeDtypeStruct((b, n, MLA_HEADS * V_HEAD), _BF16),
        scratch_shapes=[pltpu.VMEM((nk, _HEAD_SLOT), _BF16), pltpu.VMEM((V_HEAD + _SUBLANES_BF16, nk), _BF16)],
        compiler_params=_params(("parallel", "parallel", "arbitrary"), vmem),
        name="mla_attention",
    )(q, k, kr, vt)


def _rope_tables(n):
    rows_n = n // GRID_W
    row = jnp.repeat(jnp.arange(rows_n, dtype=_F32), GRID_W)
    col = jnp.tile(jnp.arange(GRID_W, dtype=_F32), rows_n)
    n_freq = QK_ROPE // 4
    inv_freq = ROPE_BASE ** (-jnp.arange(n_freq, dtype=_F32) / n_freq)
    ar, ac = row[:, None] * inv_freq, col[:, None] * inv_freq
    zeros = jnp.zeros((n, _LANES - QK_ROPE), _F32)
    cos = jnp.concatenate([jnp.cos(ar), jnp.cos(ar), jnp.cos(ac), jnp.cos(ac)], axis=1)
    sin = jnp.concatenate([-jnp.sin(ar), jnp.sin(ar), -jnp.sin(ac), jnp.sin(ac)], axis=1)
    return (jnp.concatenate([cos, zeros, zeros, cos], axis=1), jnp.concatenate([sin, zeros, zeros, sin], axis=1))


def _mla_mix(h, hc, w_a, g_q, g_kv, w_uq, w_ukv, w_o, w_o_sel, x, gate):
    b, n, d = h.shape
    nc = hc.shape[1]
    rope = _rope_tables(n)
    a_width = Q_LORA + KV_LORA + QK_ROPE
    a_pad = -(-a_width // _HEAD_SLOT) * _HEAD_SLOT
    w_a_pad = jnp.pad(w_a, ((0, 0), (0, a_pad - a_width))).astype(_BF16)[None]
    tm = _tile(n, 1024, _SUBLANES_BF16)
    a_l = _matmul(h, w_a_pad, tm=tm, tn=_HEAD_SLOT, out_dtype=_F32, name="mla_down_latent")
    a_c = _matmul(hc, w_a_pad[:, :, Q_LORA:], tm=_tile(nc, 1024, _SUBLANES_BF16), tn=_HEAD_SLOT,
                  out_dtype=_F32, name="mla_down_context")
    cq, ckv_l, kr_l = _mla_prep(a_l, g_q, g_kv, rope)
    ckv_c, kr_c = _mla_prep(a_c, None, g_kv, None)
    ckv = jnp.concatenate([ckv_c, ckv_l], axis=1)
    kr = jnp.concatenate([kr_c, kr_l], axis=1)

    head = QK_NOPE + QK_ROPE
    q, w_o_bf16 = _matmul(cq, w_uq[None], tm=tm, tn=_tile(MLA_HEADS * head, 4 * head, 2 * head), out_dtype=_BF16,
                          rope=rope, q_scale=float(head ** -0.5 * _LOG2E), side_cast=(w_o, w_o_sel),
                          name="mla_q_up")
    nkv = nc + n
    w_ukv_h = w_ukv.reshape(KV_LORA, MLA_HEADS // 2, 2, QK_NOPE + V_HEAD)
    w_uk_h = w_ukv_h[..., :QK_NOPE]
    w_uk_h = jnp.stack([w_uk_h[:, :, 0], jnp.roll(w_uk_h[:, :, 1], -(QK_NOPE // 2), axis=-1)], axis=2)
    w_uk = w_uk_h.reshape(KV_LORA, MLA_HEADS * QK_NOPE).astype(_BF16)[None]
    w_ukv_h = w_ukv_h.reshape(KV_LORA, MLA_HEADS, QK_NOPE + V_HEAD)
    w_uv_t = w_ukv_h[:, :, QK_NOPE:].reshape(KV_LORA, MLA_HEADS * V_HEAD).T.astype(_BF16)[None]
    k = _matmul(ckv, w_uk, tm=_tile(nkv, 1088, _SUBLANES_BF16), tn=_tile(MLA_HEADS * QK_NOPE, 2048, _LANES),
                out_dtype=_BF16, name="mla_k_up")
    vt = _matmul(w_uv_t, ckv, nt=True, tm=_tile(MLA_HEADS * V_HEAD, 1024, _SUBLANES_BF16),
                 tn=_tile(nkv, 2176, _LANES), out_dtype=_BF16, name="mla_v_up_transposed")
    o = _attention(q, k, kr, vt)
    return _matmul(o, w_o_bf16[None], tm=_tile(n, 512, _SUBLANES_BF16), tn=_tile(d, 512, _LANES),
                   out_dtype=_F32, resid=x, gate=gate, name="mla_out_proj")


def _ffn(x, h, w_gate, w_up, w_down, layer, gate, w_down_bf16=None):
    b, n, d = x.shape
    hidden = w_gate.shape[2]
    tm = _tile(n, 1024, _SUBLANES_BF16)
    u = _matmul(h, w_gate, b2=w_up, b_sel=layer, tm=tm, tn=_tile(hidden, 256, _LANES), out_dtype=_BF16,
                side_cast=None if w_down_bf16 is not None else (w_down, layer), name="ffn_gate_up")
    if w_down_bf16 is None:
        u, w_down_bf16 = u
    half = hidden // 2
    tk = half if half % _LANES == 0 else hidden
    out = _matmul(u, w_down_bf16[None], tm=tm, tn=_tile(d, 512, _LANES), tk=tk, out_dtype=_F32, resid=x,
                  gate=gate, name="ffn_down")
    return out, w_down_bf16


def kernel(x, c, ctx, c_ctx, w_ada, b_ada, g_mix, g_ffn, fourier_w_out, mla_w_a, mla_g_q, mla_g_kv, mla_w_uq,
           mla_w_ukv, mla_w_o, w_gate, w_up, w_down, g_final):
    b, n, d = x.shape
    depth = w_ada.shape[0]
    cond = jnp.concatenate([c, c_ctx[None]], axis=0)
    cond = jnp.pad(cond, ((0, -(b + 1) % 8), (0, 0)))
    mod = _ada_modulation(cond, w_ada, b_ada)
    dg = d // FOURIER_GROUPS
    tables = _fourier_tables(dg)

    xc = ctx
    for i in range(depth):
        need_ctx = i < depth - 1
        j = i // 2
        lat = [mod[i, :b, s * d:(s + 1) * d].reshape(b, 1, d) for s in range(N_MOD)]
        con = [mod[i, b:b + 1, s * d:(s + 1) * d].reshape(1, 1, d) for s in range(N_MOD)]
        h = _rms_norm(x, g_mix[i], lat[1], lat[0], out_dtype=_BF16)
        hc = _rms_norm(xc, g_mix[i], con[1], con[0], out_dtype=_BF16)
        tm = _tile(n, 1024, _SUBLANES_BF16)
        if i % 2 == 0:
            w_out = jnp.take(fourier_w_out[j], _mirror_row_order(d, dg), axis=0).astype(_BF16)[None]
            x = _matmul(_fourier_2d(h, tables), w_out, tm=tm, tn=_tile(d, 512, _LANES), out_dtype=_F32, resid=x,
                        gate=lat[2], name="fourier_out_proj")
            if need_ctx:
                xc = _matmul(_fourier_2d(hc, tables), w_out, tm=_tile(xc.shape[1], 1024, _SUBLANES_BF16),
                             tn=_tile(d, 512, _LANES), out_dtype=_F32, resid=xc, gate=con[2],
                             name="fourier_out_proj_ctx")
        else:
            assert not need_ctx, "context queries are only implemented for the Fourier mixer layers"
            x = _mla_mix(h, hc, mla_w_a[j], mla_g_q[j], mla_g_kv[j], mla_w_uq[j], mla_w_ukv[j], mla_w_o, j,
                         x, lat[2])
        x, w_down_i = _ffn(x, _rms_norm(x, g_ffn[i], lat[4], lat[3], out_dtype=_BF16), w_gate, w_up, w_down, i,
                           lat[5])
        if need_ctx:
            xc, _ = _ffn(xc, _rms_norm(xc, g_ffn[i], con[4], con[3], out_dtype=_BF16), w_gate, w_up, w_down, i,
                         con[5], w_down_bf16=w_down_i)
    return _rms_norm(x, g_final, out_dtype=x.dtype)
```

```python
import functools
import math

import jax
import jax.numpy as jnp
import numpy as np
from jax import lax
from jax.experimental import pallas as pl
from jax.experimental.pallas import tpu as pltpu

_F32 = jnp.float32
_BF16 = jnp.bfloat16

GRID_W = 64
FOURIER_GROUPS = 4
MLA_HEADS = 64
Q_LORA = 1536
KV_LORA = 512
QK_NOPE = 128
QK_ROPE = 64
V_HEAD = 128
ROPE_BASE = 10000.0
EPS = 1e-6
N_MOD = 6

_LANES = 128
_SUBLANES_BF16 = 16
_VMEM_CAP_BYTES = 56 * 1024 * 1024
_HEAD_SLOT = 2 * _LANES
_ATTN_KEY_CHUNK = 512
_ATTN_LOOKAHEAD = 2
_LOG2E = math.log2(math.e)


def _tile(dim, pref, align):
    if dim <= pref:
        return dim
    t = (pref // align) * align
    while t >= align:
        if dim % t == 0:
            return t
        t -= align
    return dim


def _nbytes(shape, dtype):
    return int(np.prod(shape)) * jnp.dtype(dtype).itemsize


def _params(semantics, vmem_bytes):
    return pltpu.CompilerParams(dimension_semantics=semantics,
                                vmem_limit_bytes=int(min(_VMEM_CAP_BYTES, vmem_bytes)))


def _ada_body(c_ref, w_ref, b_ref, o_ref):
    c = c_ref[...]
    s = c * jax.nn.sigmoid(c)
    acc = jnp.dot(s.astype(_BF16), w_ref[...].astype(_BF16), preferred_element_type=_F32)
    o_ref[...] = acc + b_ref[...]


def _ada_modulation(cc, w_ada, b_ada):
    depth, d, n = w_ada.shape
    r = cc.shape[0]
    tn = _tile(n, 512, _LANES)
    vmem = 2 * _nbytes((d, tn), _F32) + _nbytes((d, tn), _BF16) + 8 * _nbytes((r, tn), _F32) + (4 << 20)
    return pl.pallas_call(
        _ada_body,
        grid=(depth, n // tn),
        in_specs=[
            pl.BlockSpec((r, d), lambda l, j: (0, 0)),
            pl.BlockSpec((None, d, tn), lambda l, j: (l, 0, j)),
            pl.BlockSpec((None, 1, tn), lambda l, j: (l, 0, j)),
        ],
        out_specs=pl.BlockSpec((None, r, tn), lambda l, j: (l, 0, j)),
        out_shape=jax.ShapeDtypeStruct((depth, r, n), _F32),
        compiler_params=_params(("parallel", "parallel"), vmem),
        name="ada_modulation",
    )(cc, w_ada, b_ada.reshape(depth, 1, n))


def _norm_body(*refs, modulate):
    if modulate:
        x_ref, g_ref, sc_ref, sh_ref, o_ref = refs
    else:
        x_ref, g_ref, o_ref = refs
    x = x_ref[...]
    y = x * lax.rsqrt(jnp.mean(x * x, axis=-1, keepdims=True) + EPS) * g_ref[...]
    if modulate:
        y = y * (1.0 + sc_ref[...]) + sh_ref[...]
    o_ref[...] = y.astype(o_ref.dtype)


def _rms_norm(x, g, scale=None, shift=None, *, out_dtype):
    b, n, d = x.shape
    tr = _tile(n, 256, 8)
    modulate = scale is not None
    in_specs = [pl.BlockSpec((None, tr, d), lambda bb, i: (bb, i, 0)),
                pl.BlockSpec((1, d), lambda bb, i: (0, 0))]
    args = [x, g.reshape(1, d)]
    if modulate:
        per_batch = scale.shape[0] > 1
        mod_spec = pl.BlockSpec((None, 1, d), (lambda bb, i: (bb, 0, 0)) if per_batch else (lambda bb, i: (0, 0, 0)))
        in_specs += [mod_spec, mod_spec]
        args += [scale, shift]
    vmem = 2 * _nbytes((tr, d), _F32) + 2 * _nbytes((tr, d), out_dtype) + 3 * _nbytes((tr, d), _F32) + (2 << 20)
    return pl.pallas_call(
        functools.partial(_norm_body, modulate=modulate),
        grid=(b, n // tr),
        in_specs=in_specs,
        out_specs=pl.BlockSpec((None, tr, d), lambda bb, i: (bb, i, 0)),
        out_shape=jax.ShapeDtypeStruct((b, n, d), out_dtype),
        compiler_params=_params(("parallel", "parallel"), vmem),
        name="rms_norm_modulate" if modulate else "rms_norm",
    )(*args)


def _rope_rotate(t, cos, sin, upper_half):
    lane = lax.broadcasted_iota(jnp.int32, t.shape, 1)
    quarter = QK_ROPE // 4
    even_chunk = (lane // quarter) % 2 == 0
    swapped = jnp.where(even_chunk, pltpu.roll(t, _LANES - quarter, 1), pltpu.roll(t, quarter, 1))
    valid = (lane >= QK_ROPE) if upper_half else (lane < QK_ROPE)
    return jnp.where(valid, t * cos + swapped * sin, 0.0)


def _q_head_slots(acc, cos, sin, q_scale):
    tiles = [acc[:, i * _LANES:(i + 1) * _LANES] for i in range(acc.shape[1] // _LANES)]
    low_half = lax.broadcasted_iota(jnp.int32, tiles[0].shape, 1) < QK_ROPE
    cos_lo, cos_hi = cos[:, :_LANES], cos[:, _LANES:]
    sin_lo, sin_hi = sin[:, :_LANES], sin[:, _LANES:]
    parts = []
    for p in range(len(tiles) // 3):
        t0, t1, t2 = tiles[3 * p:3 * p + 3]
        parts += [t0 * q_scale, _rope_rotate(t1, cos_lo, sin_lo, False) * q_scale,
                  jnp.where(low_half, t2, t1) * q_scale, _rope_rotate(t2, cos_hi, sin_hi, True) * q_scale]
    return jnp.concatenate(parts, axis=1)


def _side_cast_blocks(rows, cols, steps):
    widths = [c for c in range(_LANES, cols + 1, _LANES) if cols % c == 0]
    for bc in sorted(widths, key=lambda c: abs(c - 4 * _LANES)):
        ncb = cols // bc
        if steps % ncb == 0 and rows % (steps // ncb) == 0 and (rows // (steps // ncb)) % _SUBLANES_BF16 == 0:
            return rows // (steps // ncb), bc
    raise ValueError(f"no tile-aligned split of {(rows, cols)} into {steps} blocks")


def _mm_body(*refs, n_b, epilogue, nk, q_scale, nt, side):
    a_ref = refs[0]
    b_refs = refs[1:1 + n_b]
    pos = 1 + n_b
    extra = ()
    if epilogue in ("resgate", "qrope"):
        extra = refs[pos:pos + 2]
        pos += 2
    elif epilogue == "mirror":
        extra = refs[pos:pos + 2]
        pos += 2
    if side:
        refs[pos + 2][...] = refs[pos][...].astype(refs[pos + 2].dtype)
        refs = refs[:pos] + refs[pos + 1:pos + 2] + refs[pos + 3:]
    o_ref = refs[pos]
    acc_refs = refs[pos + 1:]

    def finish(accs):
        if epilogue == "plain":
            out = accs[0]
        elif epilogue == "swiglu":
            g, u = accs
            out = g * jax.nn.sigmoid(g) * u
        elif epilogue == "resgate":
            res_ref, gate_ref = extra
            out = res_ref[...] + gate_ref[...] * accs[0]
        elif epilogue == "qrope":
            cos_ref, sin_ref = extra
            out = _q_head_slots(accs[0], cos_ref[...], sin_ref[...], q_scale)
        elif epilogue == "mirror":
            p, neg_q = accs
            e_ref, perm_ref = extra
            plus = p - neg_q
            first = plus[:, :_LANES]
            lane = lax.broadcasted_iota(jnp.int32, first.shape, 1)
            first = jnp.where(lane == 0, e_ref[...].astype(_F32), first)
            plus = jnp.concatenate([first, plus[:, _LANES:]], axis=1).astype(_BF16)
            upper = jnp.dot(plus, perm_ref[...], preferred_element_type=_F32)
            out = jnp.concatenate([p + neg_q, upper], axis=1)
        o_ref[...] = out.astype(o_ref.dtype)

    def dots():
        a = a_ref[...]
        outs = []
        for b_ref in b_refs:
            b = b_ref[...].astype(_BF16)
            if nt:
                outs.append(lax.dot_general(a, b, (((1,), (1,)), ((), ())), preferred_element_type=_F32))
            else:
                outs.append(jnp.dot(a, b, preferred_element_type=_F32))
        return outs

    if nk == 1:
        finish(dots())
        return

    k = pl.program_id(3)

    if epilogue == "mirror":
        half = nk // 2
        for idx, first_k in ((0, 0), (1, half)):
            @pl.when(k == first_k)
            def _(idx=idx):
                acc_refs[idx][...] = dots()[0]

            @pl.when((k > first_k) & (k < first_k + half))
            def _(idx=idx):
                acc_refs[idx][...] += dots()[0]
    else:
        @pl.when(k == 0)
        def _():
            for acc_ref, d in zip(acc_refs, dots()):
                acc_ref[...] = d

        @pl.when(k > 0)
        def _():
            for acc_ref, d in zip(acc_refs, dots()):
                acc_ref[...] += d

    @pl.when(k == nk - 1)
    def _():
        finish([acc_ref[...] for acc_ref in acc_refs])


def _matmul(a, b, *, tm, tn, tk=None, out_dtype, b2=None, b_sel=None, nt=False, resid=None, gate=None,
            rope=None, q_scale=1.0, side_cast=None, mirror=None, k_extent=None, name):
    ba, m, kdim = a.shape
    kdim = kdim if k_extent is None else k_extent
    lb = b.shape[0]
    n = b.shape[1] if nt else b.shape[2]
    b_batched = b_sel is None and lb > 1
    bt = max(ba, lb if b_batched else 1, resid.shape[0] if resid is not None else 1)
    tk = kdim if tk is None else tk
    nk = kdim // tk
    assert m % tm == 0 and n % tn == 0 and kdim % tk == 0, (name, a.shape, b.shape, tm, tn, tk)

    def bidx(count):
        return (lambda z: z) if count > 1 else (lambda z: 0)

    ai = bidx(ba)
    if b_sel is not None:
        bi = lambda z: b_sel
    else:
        bi = bidx(lb)
    if nt:
        b_spec = pl.BlockSpec((None, tn, tk), lambda z, i, j, k: (bi(z), j, k))
        b_block = (tn, tk)
    else:
        b_spec = pl.BlockSpec((None, tk, tn), lambda z, i, j, k: (bi(z), k, j))
        b_block = (tk, tn)
    in_specs = [pl.BlockSpec((None, tm, tk), lambda z, i, j, k: (ai(z), i, k)), b_spec]
    args = [a, b]
    n_b = 1
    epilogue = "plain"
    blocks = [((tm, tk), a.dtype), (b_block, b.dtype)]
    if b2 is not None:
        in_specs.append(b_spec)
        args.append(b2)
        blocks.append((b_block, b2.dtype))
        n_b = 2
        epilogue = "swiglu"
    if resid is not None:
        gi = bidx(gate.shape[0])
        in_specs += [pl.BlockSpec((None, tm, tn), lambda z, i, j, k: (z, i, j)),
                     pl.BlockSpec((None, 1, tn), lambda z, i, j, k: (gi(z), 0, j))]
        args += [resid, gate]
        blocks.append(((tm, tn), resid.dtype))
        epilogue = "resgate"
    out_tn, out_n = tn, n
    if rope is not None:
        head = QK_NOPE + QK_ROPE
        assert tn % (2 * head) == 0 and nk == 1
        out_tn, out_n = tn // head * _HEAD_SLOT, n // head * _HEAD_SLOT
        table_spec = pl.BlockSpec((tm, _HEAD_SLOT), lambda z, i, j, k: (i, 0))
        in_specs += [table_spec, table_spec]
        args += list(rope)
        blocks += [((tm, _HEAD_SLOT), _F32)] * 2
        epilogue = "qrope"
    n_acc = n_b
    if mirror is not None:
        assert nk % 2 == 0 and tn % _LANES == 0
        out_tn, out_n, n_acc = 2 * tn, 2 * n, 2
        unmirror = np.zeros((tn, tn), np.float32)
        unmirror[0, 0] = 1.0
        unmirror[np.arange(1, tn), tn - np.arange(1, tn)] = 1.0
        in_specs += [pl.BlockSpec((None, tm, _LANES), lambda z, i, j, k: (z, i, j)),
                     pl.BlockSpec((tn, tn), lambda z, i, j, k: (0, 0))]
        args += [mirror, jnp.asarray(unmirror, _BF16)]
        blocks += [((tm, _LANES), mirror.dtype), ((tn, tn), _BF16)]
        epilogue = "mirror"
    blocks.append(((tm, out_tn), out_dtype))
    out_specs = pl.BlockSpec((None, tm, out_tn), lambda z, i, j, k: (z, i, j))
    out_shape = jax.ShapeDtypeStruct((bt, m, out_n), out_dtype)
    if side_cast is not None:
        src, sel = side_cast
        steps_i, steps_j = m // tm, n // tn
        br, bc = _side_cast_blocks(src.shape[1], src.shape[2], bt * steps_i * steps_j)
        ncb = src.shape[2] // bc

        def step(z, i, j):
            return (z * steps_i + i) * steps_j + j

        in_specs.append(pl.BlockSpec((None, br, bc), lambda z, i, j, k: (sel, step(z, i, j) // ncb, step(z, i, j) % ncb)))
        args.append(src)
        out_specs = [out_specs, pl.BlockSpec((br, bc), lambda z, i, j, k: (step(z, i, j) // ncb, step(z, i, j) % ncb))]
        out_shape = [out_shape, jax.ShapeDtypeStruct(src.shape[1:], _BF16)]
        blocks += [((br, bc), src.dtype), ((br, bc), _BF16)]
    scratch = [pltpu.VMEM((tm, tn), _F32) for _ in range(n_acc)] if nk > 1 else []
    cast_tmp = n_b * _nbytes(b_block, _BF16) if b.dtype != _BF16 else 0
    vmem = (2 * sum(_nbytes(s, dt) for s, dt in blocks) + cast_tmp + (n_b + 2) * _nbytes((tm, tn), _F32)
            + (2 << 20))
    return pl.pallas_call(
        functools.partial(_mm_body, n_b=n_b, epilogue=epilogue, nk=nk, q_scale=q_scale, nt=nt,
                          side=side_cast is not None),
        grid=(bt, m // tm, n // tn, nk),
        in_specs=in_specs,
        out_specs=out_specs,
        out_shape=out_shape,
        scratch_shapes=scratch,
        compiler_params=_params(("parallel", "parallel", "parallel", "arbitrary"), vmem),
        name=name,
    )(*args)


def _dft_cos_sin(n, scale):
    idx = np.arange(n)
    ang = ((idx[:, None] * idx[None, :]) % n) * (2.0 * np.pi / n)
    return np.cos(ang) * scale, np.sin(ang) * scale


def _dft_gen_body(ca_ref, sa_ref, cb_ref, sb_ref, o_ref, *, n, scale):
    ca, sa, cb, sb = ca_ref[...], sa_ref[...], cb_ref[...], sb_ref[...]
    o_ref[:, :n] = ((ca * cb - sa * sb) * scale).astype(o_ref.dtype)
    o_ref[:, n:] = ((sa * cb + ca * sb) * -scale).astype(o_ref.dtype)


def _position_dft_matrix(n):
    tr = _tile(n, 256, _SUBLANES_BF16)
    t = jnp.arange(n, dtype=jnp.int32)[None, :]
    two_pi_n = 2.0 * np.pi / n
    beta = ((jnp.arange(tr, dtype=jnp.int32)[:, None] * t) % n).astype(_F32) * two_pi_n
    alpha = ((jnp.arange(0, n, tr, dtype=jnp.int32)[:, None] * t) % n).astype(_F32) * two_pi_n
    alpha = alpha.reshape(n // tr, 1, n)
    row_spec = pl.BlockSpec((None, 1, n), lambda i: (i, 0, 0))
    base_spec = pl.BlockSpec((tr, n), lambda i: (0, 0))
    vmem = 4 * _nbytes((tr, n), _F32) + 2 * _nbytes((tr, 2 * n), _BF16) + 4 * _nbytes((tr, n), _F32) + (2 << 20)
    return pl.pallas_call(
        functools.partial(_dft_gen_body, n=n, scale=float(n ** -0.5)),
        grid=(n // tr,),
        in_specs=[row_spec, row_spec, base_spec, base_spec],
        out_specs=pl.BlockSpec((tr, 2 * n), lambda i: (i, 0)),
        out_shape=jax.ShapeDtypeStruct((n, 2 * n), _BF16),
        compiler_params=_params(("parallel",), vmem),
        name="fourier_position_matrix",
    )(jnp.cos(alpha), jnp.sin(alpha), jnp.cos(beta), jnp.sin(beta))


def _channel_dft_body(h_ref, cs_ref, mid_w_ref, z_ref, mid_ref):
    h = h_ref[...]
    z_ref[...] = jnp.dot(h, cs_ref[...], preferred_element_type=_F32).astype(z_ref.dtype)

    @pl.when(pl.program_id(3) == 0)
    def _():
        mid_ref[...] = jnp.dot(h, mid_w_ref[...], preferred_element_type=_F32).astype(mid_ref.dtype)


def _channel_dft_half(h, cs_half, mid_w):
    b, n, d = h.shape
    dg, dh = cs_half.shape[1:]
    groups = d // dg
    tm = _tile(n, 1024, _SUBLANES_BF16)
    vmem = (2 * (_nbytes((tm, dg), _BF16) + _nbytes((dg, dh), _BF16) + _nbytes((dg, _LANES), _BF16)
                 + _nbytes((tm, dh), _BF16) + _nbytes((tm, _LANES), _BF16)) + 3 * _nbytes((tm, dh), _F32) + (2 << 20))
    return pl.pallas_call(
        _channel_dft_body,
        grid=(b, n // tm, groups, 2),
        in_specs=[pl.BlockSpec((None, tm, dg), lambda z, i, g, c: (z, i, g)),
                  pl.BlockSpec((None, dg, dh), lambda z, i, g, c: (c, 0, 0)),
                  pl.BlockSpec((dg, _LANES), lambda z, i, g, c: (0, 0))],
        out_specs=[pl.BlockSpec((None, None, tm, dh), lambda z, i, g, c: (z, c, i, g)),
                   pl.BlockSpec((None, tm, _LANES), lambda z, i, g, c: (z, i, g))],
        out_shape=[jax.ShapeDtypeStruct((b, 2, n, groups * dh), _BF16),
                   jax.ShapeDtypeStruct((b, n, groups * _LANES), _BF16)],
        compiler_params=_params(("parallel", "parallel", "parallel", "arbitrary"), vmem),
        name="fourier_channel_dft",
    )(h, cs_half, mid_w)


def _fourier_tables(dg):
    cc, sc = _dft_cos_sin(dg, dg ** -0.5)
    half = dg // 2
    mid_w = np.pad(cc[:, half:half + 1], ((0, 0), (0, _LANES - 1)))
    return jnp.asarray(np.stack([cc[:, :half], sc[:, :half]]), _BF16), jnp.asarray(mid_w, _BF16)


def _fourier_2d(h, tables, side_cast=None):
    b, n, d = h.shape
    cs_half, mid_w = tables
    z, mid = _channel_dft_half(h, cs_half, mid_w)
    a = _position_dft_matrix(n)[None]
    tm = _tile(n, 1024, _SUBLANES_BF16)
    e = _matmul(a, mid, tm=tm, tn=mid.shape[2], tk=_tile(n, 2048, _LANES), k_extent=n, out_dtype=_BF16,
                name="fourier_position_dft_mid")
    return _matmul(a, z.reshape(b, 2 * n, d // 2), tm=tm, tn=cs_half.shape[2], tk=_tile(n, 2048, _LANES),
                   out_dtype=_BF16, mirror=e, side_cast=side_cast, name="fourier_position_dft")


def _mla_prep_body(*refs, with_q, with_rope):
    refs = list(refs)
    a_ref = refs.pop(0)
    gq_ref = refs.pop(0) if with_q else None
    gkv_ref = refs.pop(0)
    cos_ref, sin_ref = (refs.pop(0), refs.pop(0)) if with_rope else (None, None)
    cq_ref = refs.pop(0) if with_q else None
    ckv_ref, kr_ref = refs

    def norm(t, g):
        return t * lax.rsqrt(jnp.mean(t * t, axis=-1, keepdims=True) + EPS) * g

    off = 0
    if with_q:
        cq_ref[...] = norm(a_ref[:, :Q_LORA], gq_ref[...]).astype(cq_ref.dtype)
        off = Q_LORA
    ckv_ref[...] = norm(a_ref[:, off:off + KV_LORA], gkv_ref[...]).astype(ckv_ref.dtype)
    kr = a_ref[:, off + KV_LORA:off + KV_LORA + _LANES]
    if with_rope:
        kr = _rope_rotate(kr, cos_ref[:, :_LANES], sin_ref[:, :_LANES], False)
    kr_ref[:, :_LANES] = kr.astype(kr_ref.dtype)
    kr_ref[:, _LANES:] = pltpu.roll(kr, QK_ROPE, 1).astype(kr_ref.dtype)


def _mla_prep(a, g_q, g_kv, rope):
    b, n, w = a.shape
    with_q = g_q is not None
    with_rope = rope is not None
    tr = _tile(n, 256, _SUBLANES_BF16)
    row = lambda z, i: (z, i, 0)
    in_specs = [pl.BlockSpec((None, tr, w), row)]
    args = [a]
    if with_q:
        in_specs.append(pl.BlockSpec((1, Q_LORA), lambda z, i: (0, 0)))
        args.append(g_q.reshape(1, Q_LORA))
    in_specs.append(pl.BlockSpec((1, KV_LORA), lambda z, i: (0, 0)))
    args.append(g_kv.reshape(1, KV_LORA))
    if with_rope:
        in_specs += [pl.BlockSpec((tr, _HEAD_SLOT), lambda z, i: (i, 0))] * 2
        args += list(rope)
    out_specs, out_shape = [], []
    if with_q:
        out_specs.append(pl.BlockSpec((None, tr, Q_LORA), row))
        out_shape.append(jax.ShapeDtypeStruct((b, n, Q_LORA), _BF16))
    out_specs += [pl.BlockSpec((None, tr, KV_LORA), row), pl.BlockSpec((None, tr, _HEAD_SLOT), row)]
    out_shape += [jax.ShapeDtypeStruct((b, n, KV_LORA), _BF16), jax.ShapeDtypeStruct((b, n, _HEAD_SLOT), _BF16)]
    vmem = 6 * _nbytes((tr, w), _F32) + (2 << 20)
    return pl.pallas_call(
        functools.partial(_mla_prep_body, with_q=with_q, with_rope=with_rope),
        grid=(b, n // tr),
        in_specs=in_specs,
        out_specs=out_specs,
        out_shape=out_shape,
        compiler_params=_params(("parallel", "parallel"), vmem),
        name="mla_prep_latent" if with_q else "mla_prep_context",
    )(*args)


def _attn_body(q_ref, k_ref, kr_ref, vt_ref, o_ref, kcat_ref, vext_ref, *, chunks):
    @pl.when(pl.program_id(2) == 0)
    def _():
        kcat_ref[:, :QK_NOPE] = k_ref[...]
        kcat_ref[:, QK_NOPE:] = kr_ref[...]
        vext_ref[:V_HEAD, :] = vt_ref[...]
        vext_ref[V_HEAD:, :] = jnp.ones((_SUBLANES_BF16, vext_ref.shape[1]), vext_ref.dtype)

    q = q_ref[...]

    def scores(start, size):
        return lax.dot_general(kcat_ref[start:start + size, :], q, (((1,), (1,)), ((), ())),
                               preferred_element_type=_F32)

    starts = [sum(chunks[:c]) for c in range(len(chunks))]
    m = acc = None
    ahead = [scores(starts[c], chunks[c]) for c in range(min(_ATTN_LOOKAHEAD, len(chunks)))]
    for c, (start, size) in enumerate(zip(starts, chunks)):
        st = ahead.pop(0)
        nxt = c + _ATTN_LOOKAHEAD
        if nxt < len(chunks):
            ahead.append(scores(starts[nxt], chunks[nxt]))
        mc = jnp.max(st, axis=0, keepdims=True).astype(_BF16)
        m_new = mc if m is None else jnp.maximum(m, mc)
        p = jnp.exp2(st.astype(_BF16) - m_new)
        pv = jnp.dot(vext_ref[:, start:start + size], p, preferred_element_type=_F32)
        if m is None:
            acc = pv
        else:
            acc = jnp.exp2(m.astype(_F32) - m_new.astype(_F32)) * acc + pv
        m = m_new
    o_ref[...] = (acc[:V_HEAD] / acc[V_HEAD:V_HEAD + 1]).T.astype(o_ref.dtype)


def _attention(q, k, kr, vt):
    b, n, _ = q.shape
    nk = k.shape[1]
    tq = _tile(n, 2048, _LANES)
    chunk = _ATTN_KEY_CHUNK
    chunks = [chunk] * (nk // chunk) + ([nk % chunk] if nk % chunk else [])
    vmem = (4 * _nbytes((tq, _HEAD_SLOT), _BF16) + 5 * _nbytes((nk, _LANES), _BF16)
            + 2 * _nbytes((V_HEAD, nk), _BF16) + 6 * _nbytes((min(chunk, nk), tq), _F32) + (4 << 20))
    return pl.pallas_call(
        functools.partial(_attn_body, chunks=chunks),
        grid=(b, MLA_HEADS, n // tq),
        in_specs=[pl.BlockSpec((None, tq, _HEAD_SLOT), lambda z, h, i: (z, i, h)),
                  pl.BlockSpec((None, nk, QK_NOPE), lambda z, h, i: (z, 0, h)),
                  pl.BlockSpec((None, nk, _LANES), lambda z, h, i: (z, 0, h % 2)),
                  pl.BlockSpec((None, V_HEAD, nk), lambda z, h, i: (z, h, 0))],
        out_specs=pl.BlockSpec((None, tq, V_HEAD), lambda z, h, i: (z, i, h)),
        out_shape=jax.ShapeDtypeStruct((b, n, MLA_HEADS * V_HEAD), _BF16),
        scratch_shapes=[pltpu.VMEM((nk, _HEAD_SLOT), _BF16), pltpu.VMEM((V_HEAD + _SUBLANES_BF16, nk), _BF16)],
        compiler_params=_params(("parallel", "parallel", "arbitrary"), vmem),
        name="mla_attention",
    )(q, k, kr, vt)


def _rope_tables(n):
    rows_n = n // GRID_W
    row = jnp.repeat(jnp.arange(rows_n, dtype=_F32), GRID_W)
    col = jnp.tile(jnp.arange(GRID_W, dtype=_F32), rows_n)
    n_freq = QK_ROPE // 4
    inv_freq = ROPE_BASE ** (-jnp.arange(n_freq, dtype=_F32) / n_freq)
    ar, ac = row[:, None] * inv_freq, col[:, None] * inv_freq
    zeros = jnp.zeros((n, _LANES - QK_ROPE), _F32)
    cos = jnp.concatenate([jnp.cos(ar), jnp.cos(ar), jnp.cos(ac), jnp.cos(ac)], axis=1)
    sin = jnp.concatenate([-jnp.sin(ar), jnp.sin(ar), -jnp.sin(ac), jnp.sin(ac)], axis=1)
    return (jnp.concatenate([cos, zeros, zeros, cos], axis=1), jnp.concatenate([sin, zeros, zeros, sin], axis=1))


def _mla_mix(h, hc, w_a, g_q, g_kv, w_uq, w_ukv, w_o, w_o_sel, x, gate):
    b, n, d = h.shape
    nc = hc.shape[1]
    rope = _rope_tables(n)
    a_width = Q_LORA + KV_LORA + QK_ROPE
    a_pad = -(-a_width // _HEAD_SLOT) * _HEAD_SLOT
    w_a_pad = jnp.pad(w_a, ((0, 0), (0, a_pad - a_width))).astype(_BF16)[None]
    tm = _tile(n, 1024, _SUBLANES_BF16)
    a_l = _matmul(h, w_a_pad, tm=tm, tn=_HEAD_SLOT, out_dtype=_F32, name="mla_down_latent")
    a_c = _matmul(hc, w_a_pad[:, :, Q_LORA:], tm=_tile(nc, 1024, _SUBLANES_BF16), tn=_HEAD_SLOT,
                  out_dtype=_F32, name="mla_down_context")
    cq, ckv_l, kr_l = _mla_prep(a_l, g_q, g_kv, rope)
    ckv_c, kr_c = _mla_prep(a_c, None, g_kv, None)
    ckv = jnp.concatenate([ckv_c, ckv_l], axis=1)
    kr = jnp.concatenate([kr_c, kr_l], axis=1)

    head = QK_NOPE + QK_ROPE
    q, w_o_bf16 = _matmul(cq, w_uq[None], tm=tm, tn=_tile(MLA_HEADS * head, 4 * head, 2 * head), out_dtype=_BF16,
                          rope=rope, q_scale=float(head ** -0.5 * _LOG2E), side_cast=(w_o, w_o_sel),
                          name="mla_q_up")
    nkv = nc + n
    w_ukv_h = w_ukv.reshape(KV_LORA, MLA_HEADS // 2, 2, QK_NOPE + V_HEAD)
    w_uk_h = w_ukv_h[..., :QK_NOPE]
    w_uk_h = jnp.stack([w_uk_h[:, :, 0], jnp.roll(w_uk_h[:, :, 1], -(QK_NOPE // 2), axis=-1)], axis=2)
    w_uk = w_uk_h.reshape(KV_LORA, MLA_HEADS * QK_NOPE).astype(_BF16)[None]
    w_ukv_h = w_ukv_h.reshape(KV_LORA, MLA_HEADS, QK_NOPE + V_HEAD)
    w_uv_t = w_ukv_h[:, :, QK_NOPE:].reshape(KV_LORA, MLA_HEADS * V_HEAD).T.astype(_BF16)[None]
    k = _matmul(ckv, w_uk, tm=_tile(nkv, 1088, _SUBLANES_BF16), tn=_tile(MLA_HEADS * QK_NOPE, 2048, _LANES),
                out_dtype=_BF16, name="mla_k_up")
    vt = _matmul(w_uv_t, ckv, nt=True, tm=_tile(MLA_HEADS * V_HEAD, 1024, _SUBLANES_BF16),
                 tn=_tile(nkv, 2176, _LANES), out_dtype=_BF16, name="mla_v_up_transposed")
    o = _attention(q, k, kr, vt)
    return _matmul(o, w_o_bf16[None], tm=_tile(n, 512, _SUBLANES_BF16), tn=_tile(d, 512, _LANES),
                   out_dtype=_F32, resid=x, gate=gate, name="mla_out_proj")


def _ffn(x, h, w_gate, w_up, w_down, layer, gate, w_down_bf16=None):
    b, n, d = x.shape
    hidden = w_gate.shape[2]
    tm = _tile(n, 1024, _SUBLANES_BF16)
    u = _matmul(h, w_gate, b2=w_up, b_sel=layer, tm=tm, tn=_tile(hidden, 256, _LANES), out_dtype=_BF16,
                side_cast=None if w_down_bf16 is not None else (w_down, layer), name="ffn_gate_up")
    if w_down_bf16 is None:
        u, w_down_bf16 = u
    half = hidden // 2
    tk = half if half % _LANES == 0 else hidden
    out = _matmul(u, w_down_bf16[None], tm=tm, tn=_tile(d, 512, _LANES), tk=tk, out_dtype=_F32, resid=x,
                  gate=gate, name="ffn_down")
    return out, w_down_bf16


def kernel(x, c, ctx, c_ctx, w_ada, b_ada, g_mix, g_ffn, fourier_w_out, mla_w_a, mla_g_q, mla_g_kv, mla_w_uq,
           mla_w_ukv, mla_w_o, w_gate, w_up, w_down, g_final):
    b, n, d = x.shape
    depth = w_ada.shape[0]
    cond = jnp.concatenate([c, c_ctx[None]], axis=0)
    cond = jnp.pad(cond, ((0, -(b + 1) % 8), (0, 0)))
    mod = _ada_modulation(cond, w_ada, b_ada)
    dg = d // FOURIER_GROUPS
    tables = _fourier_tables(dg)

    xc = ctx
    for i in range(depth):
        need_ctx = i < depth - 1
        j = i // 2
        lat = [mod[i, :b, s * d:(s + 1) * d].reshape(b, 1, d) for s in range(N_MOD)]
        con = [mod[i, b:b + 1, s * d:(s + 1) * d].reshape(1, 1, d) for s in range(N_MOD)]
        h = _rms_norm(x, g_mix[i], lat[1], lat[0], out_dtype=_BF16)
        hc = _rms_norm(xc, g_mix[i], con[1], con[0], out_dtype=_BF16)
        tm = _tile(n, 1024, _SUBLANES_BF16)
        if i % 2 == 0:
            f, w_out = _fourier_2d(h, tables, side_cast=(fourier_w_out, j))
            w_out = w_out[None]
            x = _matmul(f, w_out, tm=tm, tn=_tile(d, 512, _LANES), out_dtype=_F32, resid=x, gate=lat[2],
                        name="fourier_out_proj")
            if need_ctx:
                xc = _matmul(_fourier_2d(hc, tables), w_out, tm=_tile(xc.shape[1], 1024, _SUBLANES_BF16),
                             tn=_tile(d, 512, _LANES), out_dtype=_F32, resid=xc, gate=con[2],
                             name="fourier_out_proj_ctx")
        else:
            assert not need_ctx, "context queries are only implemented for the Fourier mixer layers"
            x = _mla_mix(h, hc, mla_w_a[j], mla_g_q[j], mla_g_kv[j], mla_w_uq[j], mla_w_ukv[j], mla_w_o, j,
                         x, lat[2])
        x, w_down_i = _ffn(x, _rms_norm(x, g_ffn[i], lat[4], lat[3], out_dtype=_BF16), w_gate, w_up, w_down, i,
                           lat[5])
        if need_ctx:
            hc2 = _rms_norm(xc, g_ffn[i], con[4], con[3], out_dtype=_BF16)
            nc = xc.shape[1]
            xc, _ = _ffn(xc.reshape(1, b * nc, d), hc2.reshape(1, b * nc, d), w_gate, w_up, w_down, i, con[5],
                         w_down_bf16=w_down_i)
            xc = xc.reshape(b, nc, d)
    return _rms_norm(x, g_final, out_dtype=x.dtype)
```

```python
import functools
import math

import jax
import jax.numpy as jnp
import numpy as np
from jax import lax
from jax.experimental import pallas as pl
from jax.experimental.pallas import tpu as pltpu

_F32 = jnp.float32
_BF16 = jnp.bfloat16

GRID_W = 64
FOURIER_GROUPS = 4
MLA_HEADS = 64
Q_LORA = 1536
KV_LORA = 512
QK_NOPE = 128
QK_ROPE = 64
V_HEAD = 128
ROPE_BASE = 10000.0
EPS = 1e-6
N_MOD = 6

_LANES = 128
_SUBLANES_BF16 = 16
_VMEM_CAP_BYTES = 56 * 1024 * 1024
_HEAD_SLOT = 2 * _LANES
_ATTN_KEY_CHUNK = 512
_ATTN_LOOKAHEAD = 2
_LOG2E = math.log2(math.e)


def _tile(dim, pref, align):
    if dim <= pref:
        return dim
    t = (pref // align) * align
    while t >= align:
        if dim % t == 0:
            return t
        t -= align
    return dim


def _nbytes(shape, dtype):
    return int(np.prod(shape)) * jnp.dtype(dtype).itemsize


def _params(semantics, vmem_bytes):
    return pltpu.CompilerParams(dimension_semantics=semantics,
                                vmem_limit_bytes=int(min(_VMEM_CAP_BYTES, vmem_bytes)))


def _ada_body(c_ref, w_ref, b_ref, o_ref):
    c = c_ref[...]
    s = c * jax.nn.sigmoid(c)
    acc = jnp.dot(s.astype(_BF16), w_ref[...].astype(_BF16), preferred_element_type=_F32)
    o_ref[...] = acc + b_ref[...]


def _ada_modulation(cc, w_ada, b_ada):
    depth, d, n = w_ada.shape
    r = cc.shape[0]
    tn = _tile(n, 512, _LANES)
    vmem = 2 * _nbytes((d, tn), _F32) + _nbytes((d, tn), _BF16) + 8 * _nbytes((r, tn), _F32) + (4 << 20)
    return pl.pallas_call(
        _ada_body,
        grid=(depth, n // tn),
        in_specs=[
            pl.BlockSpec((r, d), lambda l, j: (0, 0)),
            pl.BlockSpec((None, d, tn), lambda l, j: (l, 0, j)),
            pl.BlockSpec((None, 1, tn), lambda l, j: (l, 0, j)),
        ],
        out_specs=pl.BlockSpec((None, r, tn), lambda l, j: (l, 0, j)),
        out_shape=jax.ShapeDtypeStruct((depth, r, n), _F32),
        compiler_params=_params(("parallel", "parallel"), vmem),
        name="ada_modulation",
    )(cc, w_ada, b_ada.reshape(depth, 1, n))


def _norm_body(*refs, modulate):
    if modulate:
        x_ref, g_ref, sc_ref, sh_ref, o_ref = refs
    else:
        x_ref, g_ref, o_ref = refs
    x = x_ref[...]
    y = x * lax.rsqrt(jnp.mean(x * x, axis=-1, keepdims=True) + EPS) * g_ref[...]
    if modulate:
        y = y * (1.0 + sc_ref[...]) + sh_ref[...]
    o_ref[...] = y.astype(o_ref.dtype)


def _rms_norm(x, g, scale=None, shift=None, *, out_dtype):
    b, n, d = x.shape
    tr = _tile(n, 256, 8)
    modulate = scale is not None
    in_specs = [pl.BlockSpec((None, tr, d), lambda bb, i: (bb, i, 0)),
                pl.BlockSpec((1, d), lambda bb, i: (0, 0))]
    args = [x, g.reshape(1, d)]
    if modulate:
        per_batch = scale.shape[0] > 1
        mod_spec = pl.BlockSpec((None, 1, d), (lambda bb, i: (bb, 0, 0)) if per_batch else (lambda bb, i: (0, 0, 0)))
        in_specs += [mod_spec, mod_spec]
        args += [scale, shift]
    vmem = 2 * _nbytes((tr, d), _F32) + 2 * _nbytes((tr, d), out_dtype) + 3 * _nbytes((tr, d), _F32) + (2 << 20)
    return pl.pallas_call(
        functools.partial(_norm_body, modulate=modulate),
        grid=(b, n // tr),
        in_specs=in_specs,
        out_specs=pl.BlockSpec((None, tr, d), lambda bb, i: (bb, i, 0)),
        out_shape=jax.ShapeDtypeStruct((b, n, d), out_dtype),
        compiler_params=_params(("parallel", "parallel"), vmem),
        name="rms_norm_modulate" if modulate else "rms_norm",
    )(*args)


def _rope_rotate(t, cos, sin, upper_half):
    lane = lax.broadcasted_iota(jnp.int32, t.shape, 1)
    quarter = QK_ROPE // 4
    even_chunk = (lane // quarter) % 2 == 0
    swapped = jnp.where(even_chunk, pltpu.roll(t, _LANES - quarter, 1), pltpu.roll(t, quarter, 1))
    valid = (lane >= QK_ROPE) if upper_half else (lane < QK_ROPE)
    return jnp.where(valid, t * cos + swapped * sin, 0.0)


def _q_head_slots(acc, cos, sin, q_scale):
    tiles = [acc[:, i * _LANES:(i + 1) * _LANES] for i in range(acc.shape[1] // _LANES)]
    low_half = lax.broadcasted_iota(jnp.int32, tiles[0].shape, 1) < QK_ROPE
    cos_lo, cos_hi = cos[:, :_LANES], cos[:, _LANES:]
    sin_lo, sin_hi = sin[:, :_LANES], sin[:, _LANES:]
    parts = []
    for p in range(len(tiles) // 3):
        t0, t1, t2 = tiles[3 * p:3 * p + 3]
        parts += [t0 * q_scale, _rope_rotate(t1, cos_lo, sin_lo, False) * q_scale,
                  jnp.where(low_half, t2, t1) * q_scale, _rope_rotate(t2, cos_hi, sin_hi, True) * q_scale]
    return jnp.concatenate(parts, axis=1)


def _side_cast_blocks(rows, cols, steps):
    widths = [c for c in range(_LANES, cols + 1, _LANES) if cols % c == 0]
    for bc in sorted(widths, key=lambda c: abs(c - 4 * _LANES)):
        ncb = cols // bc
        if steps % ncb == 0 and rows % (steps // ncb) == 0 and (rows // (steps // ncb)) % _SUBLANES_BF16 == 0:
            return rows // (steps // ncb), bc
    raise ValueError(f"no tile-aligned split of {(rows, cols)} into {steps} blocks")


def _mm_body(*refs, n_b, epilogue, nk, q_scale, nt, side):
    a_ref = refs[0]
    b_refs = refs[1:1 + n_b]
    pos = 1 + n_b
    extra = ()
    if epilogue in ("resgate", "qrope"):
        extra = refs[pos:pos + 2]
        pos += 2
    elif epilogue == "mirror":
        extra = refs[pos:pos + 2]
        pos += 2
    if side:
        refs[pos + 2][...] = refs[pos][...].astype(refs[pos + 2].dtype)
        refs = refs[:pos] + refs[pos + 1:pos + 2] + refs[pos + 3:]
    o_ref = refs[pos]
    acc_refs = refs[pos + 1:]

    def finish(accs):
        if epilogue == "plain":
            out = accs[0]
        elif epilogue == "swiglu":
            g, u = accs
            out = g * jax.nn.sigmoid(g) * u
        elif epilogue == "resgate":
            res_ref, gate_ref = extra
            out = res_ref[...] + gate_ref[...] * accs[0]
        elif epilogue == "qrope":
            cos_ref, sin_ref = extra
            out = _q_head_slots(accs[0], cos_ref[...], sin_ref[...], q_scale)
        elif epilogue == "mirror":
            p, neg_q = accs
            e_ref, perm_ref = extra
            plus = p - neg_q
            first = plus[:, :_LANES]
            lane = lax.broadcasted_iota(jnp.int32, first.shape, 1)
            first = jnp.where(lane == 0, e_ref[...].astype(_F32), first)
            plus = jnp.concatenate([first, plus[:, _LANES:]], axis=1).astype(_BF16)
            upper = jnp.dot(plus, perm_ref[...], preferred_element_type=_F32)
            out = jnp.concatenate([p + neg_q, upper], axis=1)
        o_ref[...] = out.astype(o_ref.dtype)

    def dots():
        a = a_ref[...]
        outs = []
        for b_ref in b_refs:
            b = b_ref[...].astype(_BF16)
            if nt:
                outs.append(lax.dot_general(a, b, (((1,), (1,)), ((), ())), preferred_element_type=_F32))
            else:
                outs.append(jnp.dot(a, b, preferred_element_type=_F32))
        return outs

    if nk == 1:
        finish(dots())
        return

    k = pl.program_id(3)

    if epilogue == "mirror":
        half = nk // 2
        for idx, first_k in ((0, 0), (1, half)):
            @pl.when(k == first_k)
            def _(idx=idx):
                acc_refs[idx][...] = dots()[0]

            @pl.when((k > first_k) & (k < first_k + half))
            def _(idx=idx):
                acc_refs[idx][...] += dots()[0]
    else:
        @pl.when(k == 0)
        def _():
            for acc_ref, d in zip(acc_refs, dots()):
                acc_ref[...] = d

        @pl.when(k > 0)
        def _():
            for acc_ref, d in zip(acc_refs, dots()):
                acc_ref[...] += d

    @pl.when(k == nk - 1)
    def _():
        finish([acc_ref[...] for acc_ref in acc_refs])


def _matmul(a, b, *, tm, tn, tk=None, out_dtype, b2=None, b_sel=None, nt=False, resid=None, gate=None,
            rope=None, q_scale=1.0, side_cast=None, mirror=None, k_extent=None, name):
    ba, m, kdim = a.shape
    kdim = kdim if k_extent is None else k_extent
    lb = b.shape[0]
    n = b.shape[1] if nt else b.shape[2]
    b_batched = b_sel is None and lb > 1
    bt = max(ba, lb if b_batched else 1, resid.shape[0] if resid is not None else 1)
    tk = kdim if tk is None else tk
    nk = kdim // tk
    assert m % tm == 0 and n % tn == 0 and kdim % tk == 0, (name, a.shape, b.shape, tm, tn, tk)

    def bidx(count):
        return (lambda z: z) if count > 1 else (lambda z: 0)

    ai = bidx(ba)
    if b_sel is not None:
        bi = lambda z: b_sel
    else:
        bi = bidx(lb)
    if nt:
        b_spec = pl.BlockSpec((None, tn, tk), lambda z, i, j, k: (bi(z), j, k))
        b_block = (tn, tk)
    else:
        b_spec = pl.BlockSpec((None, tk, tn), lambda z, i, j, k: (bi(z), k, j))
        b_block = (tk, tn)
    in_specs = [pl.BlockSpec((None, tm, tk), lambda z, i, j, k: (ai(z), i, k)), b_spec]
    args = [a, b]
    n_b = 1
    epilogue = "plain"
    blocks = [((tm, tk), a.dtype), (b_block, b.dtype)]
    if b2 is not None:
        in_specs.append(b_spec)
        args.append(b2)
        blocks.append((b_block, b2.dtype))
        n_b = 2
        epilogue = "swiglu"
    if resid is not None:
        gi = bidx(gate.shape[0])
        in_specs += [pl.BlockSpec((None, tm, tn), lambda z, i, j, k: (z, i, j)),
                     pl.BlockSpec((None, 1, tn), lambda z, i, j, k: (gi(z), 0, j))]
        args += [resid, gate]
        blocks.append(((tm, tn), resid.dtype))
        epilogue = "resgate"
    out_tn, out_n = tn, n
    if rope is not None:
        head = QK_NOPE + QK_ROPE
        assert tn % (2 * head) == 0 and nk == 1
        out_tn, out_n = tn // head * _HEAD_SLOT, n // head * _HEAD_SLOT
        table_spec = pl.BlockSpec((tm, _HEAD_SLOT), lambda z, i, j, k: (i, 0))
        in_specs += [table_spec, table_spec]
        args += list(rope)
        blocks += [((tm, _HEAD_SLOT), _F32)] * 2
        epilogue = "qrope"
    n_acc = n_b
    if mirror is not None:
        assert nk % 2 == 0 and tn % _LANES == 0
        out_tn, out_n, n_acc = 2 * tn, 2 * n, 2
        unmirror = np.zeros((tn, tn), np.float32)
        unmirror[0, 0] = 1.0
        unmirror[np.arange(1, tn), tn - np.arange(1, tn)] = 1.0
        in_specs += [pl.BlockSpec((None, tm, _LANES), lambda z, i, j, k: (z, i, j)),
                     pl.BlockSpec((tn, tn), lambda z, i, j, k: (0, 0))]
        args += [mirror, jnp.asarray(unmirror, _BF16)]
        blocks += [((tm, _LANES), mirror.dtype), ((tn, tn), _BF16)]
        epilogue = "mirror"
    blocks.append(((tm, out_tn), out_dtype))
    out_specs = pl.BlockSpec((None, tm, out_tn), lambda z, i, j, k: (z, i, j))
    out_shape = jax.ShapeDtypeStruct((bt, m, out_n), out_dtype)
    if side_cast is not None:
        src, sel = side_cast
        steps_i, steps_j = m // tm, n // tn
        br, bc = _side_cast_blocks(src.shape[1], src.shape[2], bt * steps_i * steps_j)
        ncb = src.shape[2] // bc

        def step(z, i, j):
            return (z * steps_i + i) * steps_j + j

        in_specs.append(pl.BlockSpec((None, br, bc), lambda z, i, j, k: (sel, step(z, i, j) // ncb, step(z, i, j) % ncb)))
        args.append(src)
        out_specs = [out_specs, pl.BlockSpec((br, bc), lambda z, i, j, k: (step(z, i, j) // ncb, step(z, i, j) % ncb))]
        out_shape = [out_shape, jax.ShapeDtypeStruct(src.shape[1:], _BF16)]
        blocks += [((br, bc), src.dtype), ((br, bc), _BF16)]
    scratch = [pltpu.VMEM((tm, tn), _F32) for _ in range(n_acc)] if nk > 1 else []
    cast_tmp = n_b * _nbytes(b_block, _BF16) if b.dtype != _BF16 else 0
    vmem = (2 * sum(_nbytes(s, dt) for s, dt in blocks) + cast_tmp + (n_b + 2) * _nbytes((tm, tn), _F32)
            + (2 << 20))
    return pl.pallas_call(
        functools.partial(_mm_body, n_b=n_b, epilogue=epilogue, nk=nk, q_scale=q_scale, nt=nt,
                          side=side_cast is not None),
        grid=(bt, m // tm, n // tn, nk),
        in_specs=in_specs,
        out_specs=out_specs,
        out_shape=out_shape,
        scratch_shapes=scratch,
        compiler_params=_params(("parallel", "parallel", "parallel", "arbitrary"), vmem),
        name=name,
    )(*args)


def _dft_cos_sin(n, scale):
    idx = np.arange(n)
    ang = ((idx[:, None] * idx[None, :]) % n) * (2.0 * np.pi / n)
    return np.cos(ang) * scale, np.sin(ang) * scale


def _dft_gen_body(ca_ref, sa_ref, cb_ref, sb_ref, o_ref, *, n, scale):
    ca, sa, cb, sb = ca_ref[...], sa_ref[...], cb_ref[...], sb_ref[...]
    o_ref[:, :n] = ((ca * cb - sa * sb) * scale).astype(o_ref.dtype)
    o_ref[:, n:] = ((sa * cb + ca * sb) * -scale).astype(o_ref.dtype)


def _position_dft_matrix(n):
    tr = _tile(n, 256, _SUBLANES_BF16)
    t = jnp.arange(n, dtype=jnp.int32)[None, :]
    two_pi_n = 2.0 * np.pi / n
    beta = ((jnp.arange(tr, dtype=jnp.int32)[:, None] * t) % n).astype(_F32) * two_pi_n
    alpha = ((jnp.arange(0, n, tr, dtype=jnp.int32)[:, None] * t) % n).astype(_F32) * two_pi_n
    alpha = alpha.reshape(n // tr, 1, n)
    row_spec = pl.BlockSpec((None, 1, n), lambda i: (i, 0, 0))
    base_spec = pl.BlockSpec((tr, n), lambda i: (0, 0))
    vmem = 4 * _nbytes((tr, n), _F32) + 2 * _nbytes((tr, 2 * n), _BF16) + 4 * _nbytes((tr, n), _F32) + (2 << 20)
    return pl.pallas_call(
        functools.partial(_dft_gen_body, n=n, scale=float(n ** -0.5)),
        grid=(n // tr,),
        in_specs=[row_spec, row_spec, base_spec, base_spec],
        out_specs=pl.BlockSpec((tr, 2 * n), lambda i: (i, 0)),
        out_shape=jax.ShapeDtypeStruct((n, 2 * n), _BF16),
        compiler_params=_params(("parallel",), vmem),
        name="fourier_position_matrix",
    )(jnp.cos(alpha), jnp.sin(alpha), jnp.cos(beta), jnp.sin(beta))


def _channel_dft_body(h_ref, cs_ref, mid_w_ref, z_ref, mid_ref):
    h = h_ref[...]
    z_ref[...] = jnp.dot(h, cs_ref[...], preferred_element_type=_F32).astype(z_ref.dtype)

    @pl.when(pl.program_id(3) == 0)
    def _():
        mid_ref[...] = jnp.dot(h, mid_w_ref[...], preferred_element_type=_F32).astype(mid_ref.dtype)


def _channel_dft_half(h, cs_half, mid_w):
    b, n, d = h.shape
    dg, dh = cs_half.shape[1:]
    groups = d // dg
    tm = _tile(n, 1024, _SUBLANES_BF16)
    vmem = (2 * (_nbytes((tm, dg), _BF16) + _nbytes((dg, dh), _BF16) + _nbytes((dg, _LANES), _BF16)
                 + _nbytes((tm, dh), _BF16) + _nbytes((tm, _LANES), _BF16)) + 3 * _nbytes((tm, dh), _F32) + (2 << 20))
    return pl.pallas_call(
        _channel_dft_body,
        grid=(b, n // tm, groups, 2),
        in_specs=[pl.BlockSpec((None, tm, dg), lambda z, i, g, c: (z, i, g)),
                  pl.BlockSpec((None, dg, dh), lambda z, i, g, c: (c, 0, 0)),
                  pl.BlockSpec((dg, _LANES), lambda z, i, g, c: (0, 0))],
        out_specs=[pl.BlockSpec((None, None, tm, dh), lambda z, i, g, c: (z, c, i, g)),
                   pl.BlockSpec((None, tm, _LANES), lambda z, i, g, c: (z, i, g))],
        out_shape=[jax.ShapeDtypeStruct((b, 2, n, groups * dh), _BF16),
                   jax.ShapeDtypeStruct((b, n, groups * _LANES), _BF16)],
        compiler_params=_params(("parallel", "parallel", "parallel", "arbitrary"), vmem),
        name="fourier_channel_dft",
    )(h, cs_half, mid_w)


def _fourier_tables(dg):
    cc, sc = _dft_cos_sin(dg, dg ** -0.5)
    half = dg // 2
    mid_w = np.pad(cc[:, half:half + 1], ((0, 0), (0, _LANES - 1)))
    return jnp.asarray(np.stack([cc[:, :half], sc[:, :half]]), _BF16), jnp.asarray(mid_w, _BF16)


def _fourier_2d(h, tables, side_cast=None):
    b, n, d = h.shape
    cs_half, mid_w = tables
    z, mid = _channel_dft_half(h, cs_half, mid_w)
    a = _position_dft_matrix(n)[None]
    tm = _tile(n, 1024, _SUBLANES_BF16)
    e = _matmul(a, mid, tm=tm, tn=mid.shape[2], tk=_tile(n, 2048, _LANES), k_extent=n, out_dtype=_BF16,
                name="fourier_position_dft_mid")
    return _matmul(a, z.reshape(b, 2 * n, d // 2), tm=tm, tn=cs_half.shape[2], tk=_tile(n, 2048, _LANES),
                   out_dtype=_BF16, mirror=e, side_cast=side_cast, name="fourier_position_dft")


def _mla_prep_body(*refs, with_q, with_rope):
    refs = list(refs)
    a_ref = refs.pop(0)
    gq_ref = refs.pop(0) if with_q else None
    gkv_ref = refs.pop(0)
    cos_ref, sin_ref = (refs.pop(0), refs.pop(0)) if with_rope else (None, None)
    cq_ref = refs.pop(0) if with_q else None
    ckv_ref, kr_ref = refs

    def norm(t, g):
        return t * lax.rsqrt(jnp.mean(t * t, axis=-1, keepdims=True) + EPS) * g

    off = 0
    if with_q:
        cq_ref[...] = norm(a_ref[:, :Q_LORA], gq_ref[...]).astype(cq_ref.dtype)
        off = Q_LORA
    ckv_ref[...] = norm(a_ref[:, off:off + KV_LORA], gkv_ref[...]).astype(ckv_ref.dtype)
    kr = a_ref[:, off + KV_LORA:off + KV_LORA + _LANES]
    if with_rope:
        kr = _rope_rotate(kr, cos_ref[:, :_LANES], sin_ref[:, :_LANES], False)
    kr_ref[:, :_LANES] = kr.astype(kr_ref.dtype)
    kr_ref[:, _LANES:] = pltpu.roll(kr, QK_ROPE, 1).astype(kr_ref.dtype)


def _mla_prep(a, g_q, g_kv, rope):
    b, n, w = a.shape
    with_q = g_q is not None
    with_rope = rope is not None
    tr = _tile(n, 256, _SUBLANES_BF16)
    row = lambda z, i: (z, i, 0)
    in_specs = [pl.BlockSpec((None, tr, w), row)]
    args = [a]
    if with_q:
        in_specs.append(pl.BlockSpec((1, Q_LORA), lambda z, i: (0, 0)))
        args.append(g_q.reshape(1, Q_LORA))
    in_specs.append(pl.BlockSpec((1, KV_LORA), lambda z, i: (0, 0)))
    args.append(g_kv.reshape(1, KV_LORA))
    if with_rope:
        in_specs += [pl.BlockSpec((tr, _HEAD_SLOT), lambda z, i: (i, 0))] * 2
        args += list(rope)
    out_specs, out_shape = [], []
    if with_q:
        out_specs.append(pl.BlockSpec((None, tr, Q_LORA), row))
        out_shape.append(jax.ShapeDtypeStruct((b, n, Q_LORA), _BF16))
    out_specs += [pl.BlockSpec((None, tr, KV_LORA), row), pl.BlockSpec((None, tr, _HEAD_SLOT), row)]
    out_shape += [jax.ShapeDtypeStruct((b, n, KV_LORA), _BF16), jax.ShapeDtypeStruct((b, n, _HEAD_SLOT), _BF16)]
    vmem = 6 * _nbytes((tr, w), _F32) + (2 << 20)
    return pl.pallas_call(
        functools.partial(_mla_prep_body, with_q=with_q, with_rope=with_rope),
        grid=(b, n // tr),
        in_specs=in_specs,
        out_specs=out_specs,
        out_shape=out_shape,
        compiler_params=_params(("parallel", "parallel"), vmem),
        name="mla_prep_latent" if with_q else "mla_prep_context",
    )(*args)


def _attn_body(q_ref, k_ref, kr_ref, vt_ref, o_ref, kcat_ref, vext_ref, *, chunks):
    @pl.when(pl.program_id(2) == 0)
    def _():
        kcat_ref[:, :QK_NOPE] = k_ref[...]
        kcat_ref[:, QK_NOPE:] = kr_ref[...]
        vext_ref[:V_HEAD, :] = vt_ref[...]
        vext_ref[V_HEAD:, :] = jnp.ones((_SUBLANES_BF16, vext_ref.shape[1]), vext_ref.dtype)

    q = q_ref[...]

    def scores(start, size):
        return lax.dot_general(kcat_ref[start:start + size, :], q, (((1,), (1,)), ((), ())),
                               preferred_element_type=_F32)

    starts = [sum(chunks[:c]) for c in range(len(chunks))]
    m = acc = None
    ahead = [scores(starts[c], chunks[c]) for c in range(min(_ATTN_LOOKAHEAD, len(chunks)))]
    for c, (start, size) in enumerate(zip(starts, chunks)):
        st = ahead.pop(0)
        nxt = c + _ATTN_LOOKAHEAD
        if nxt < len(chunks):
            ahead.append(scores(starts[nxt], chunks[nxt]))
        mc = jnp.max(st, axis=0, keepdims=True).astype(_BF16)
        m_new = mc if m is None else jnp.maximum(m, mc)
        p = jnp.exp2(st.astype(_BF16) - m_new)
        pv = jnp.dot(vext_ref[:, start:start + size], p, preferred_element_type=_F32)
        if m is None:
            acc = pv
        else:
            acc = jnp.exp2(m.astype(_F32) - m_new.astype(_F32)) * acc + pv
        m = m_new
    o_ref[...] = (acc[:V_HEAD] / acc[V_HEAD:V_HEAD + 1]).T.astype(o_ref.dtype)


def _attention(q, k, kr, vt):
    b, n, _ = q.shape
    nk = k.shape[1]
    tq = _tile(n, 4096, _LANES)
    chunk = _ATTN_KEY_CHUNK
    chunks = [chunk] * (nk // chunk) + ([nk % chunk] if nk % chunk else [])
    vmem = (4 * _nbytes((tq, _HEAD_SLOT), _BF16) + 5 * _nbytes((nk, _LANES), _BF16)
            + 2 * _nbytes((V_HEAD, nk), _BF16) + 6 * _nbytes((min(chunk, nk), tq), _F32) + (4 << 20))
    return pl.pallas_call(
        functools.partial(_attn_body, chunks=chunks),
        grid=(b, MLA_HEADS, n // tq),
        in_specs=[pl.BlockSpec((None, tq, _HEAD_SLOT), lambda z, h, i: (z, i, h)),
                  pl.BlockSpec((None, nk, QK_NOPE), lambda z, h, i: (z, 0, h)),
                  pl.BlockSpec((None, nk, _LANES), lambda z, h, i: (z, 0, h % 2)),
                  pl.BlockSpec((None, V_HEAD, nk), lambda z, h, i: (z, h, 0))],
        out_specs=pl.BlockSpec((None, tq, V_HEAD), lambda z, h, i: (z, i, h)),
        out_shape=jax.ShapeDtypeStruct((b, n, MLA_HEADS * V_HEAD), _BF16),
        scratch_shapes=[pltpu.VMEM((nk, _HEAD_SLOT), _BF16), pltpu.VMEM((V_HEAD + _SUBLANES_BF16, nk), _BF16)],
        compiler_params=_params(("parallel", "parallel", "arbitrary"), vmem),
        name="mla_attention",
    )(q, k, kr, vt)


def _rope_tables(n):
    rows_n = n // GRID_W
    row = jnp.repeat(jnp.arange(rows_n, dtype=_F32), GRID_W)
    col = jnp.tile(jnp.arange(GRID_W, dtype=_F32), rows_n)
    n_freq = QK_ROPE // 4
    inv_freq = ROPE_BASE ** (-jnp.arange(n_freq, dtype=_F32) / n_freq)
    ar, ac = row[:, None] * inv_freq, col[:, None] * inv_freq
    zeros = jnp.zeros((n, _LANES - QK_ROPE), _F32)
    cos = jnp.concatenate([jnp.cos(ar), jnp.cos(ar), jnp.cos(ac), jnp.cos(ac)], axis=1)
    sin = jnp.concatenate([-jnp.sin(ar), jnp.sin(ar), -jnp.sin(ac), jnp.sin(ac)], axis=1)
    return (jnp.concatenate([cos, zeros, zeros, cos], axis=1), jnp.concatenate([sin, zeros, zeros, sin], axis=1))


def _mla_mix(h, hc, w_a, g_q, g_kv, w_uq, w_ukv, w_o, w_o_sel, x, gate):
    b, n, d = h.shape
    nc = hc.shape[1]
    rope = _rope_tables(n)
    a_width = Q_LORA + KV_LORA + QK_ROPE
    a_pad = -(-a_width // _HEAD_SLOT) * _HEAD_SLOT
    w_a_pad = jnp.pad(w_a, ((0, 0), (0, a_pad - a_width))).astype(_BF16)[None]
    tm = _tile(n, 1024, _SUBLANES_BF16)
    a_l = _matmul(h, w_a_pad, tm=tm, tn=_tile(a_pad, 3 * _HEAD_SLOT, _HEAD_SLOT), out_dtype=_F32,
                  name="mla_down_latent")
    a_c = _matmul(hc, w_a_pad[:, :, Q_LORA:], tm=_tile(nc, 1024, _SUBLANES_BF16), tn=_HEAD_SLOT,
                  out_dtype=_F32, name="mla_down_context")
    cq, ckv_l, kr_l = _mla_prep(a_l, g_q, g_kv, rope)
    ckv_c, kr_c = _mla_prep(a_c, None, g_kv, None)
    ckv = jnp.concatenate([ckv_c, ckv_l], axis=1)
    kr = jnp.concatenate([kr_c, kr_l], axis=1)

    head = QK_NOPE + QK_ROPE
    q, w_o_bf16 = _matmul(cq, w_uq[None], tm=tm, tn=_tile(MLA_HEADS * head, 4 * head, 2 * head), out_dtype=_BF16,
                          rope=rope, q_scale=float(head ** -0.5 * _LOG2E), side_cast=(w_o, w_o_sel),
                          name="mla_q_up")
    nkv = nc + n
    w_ukv_h = w_ukv.reshape(KV_LORA, MLA_HEADS // 2, 2, QK_NOPE + V_HEAD)
    w_uk_h = w_ukv_h[..., :QK_NOPE]
    w_uk_h = jnp.stack([w_uk_h[:, :, 0], jnp.roll(w_uk_h[:, :, 1], -(QK_NOPE // 2), axis=-1)], axis=2)
    w_uk = w_uk_h.reshape(KV_LORA, MLA_HEADS * QK_NOPE).astype(_BF16)[None]
    w_ukv_h = w_ukv_h.reshape(KV_LORA, MLA_HEADS, QK_NOPE + V_HEAD)
    w_uv_t = w_ukv_h[:, :, QK_NOPE:].reshape(KV_LORA, MLA_HEADS * V_HEAD).T.astype(_BF16)[None]
    k = _matmul(ckv, w_uk, tm=_tile(nkv, 1088, _SUBLANES_BF16), tn=_tile(MLA_HEADS * QK_NOPE, 2048, _LANES),
                out_dtype=_BF16, name="mla_k_up")
    vt = _matmul(w_uv_t, ckv, nt=True, tm=_tile(MLA_HEADS * V_HEAD, 1024, _SUBLANES_BF16),
                 tn=_tile(nkv, 2176, _LANES), out_dtype=_BF16, name="mla_v_up_transposed")
    o = _attention(q, k, kr, vt)
    return _matmul(o, w_o_bf16[None], tm=_tile(n, 512, _SUBLANES_BF16), tn=_tile(d, 512, _LANES),
                   out_dtype=_F32, resid=x, gate=gate, name="mla_out_proj")


def _ffn(x, h, w_gate, w_up, w_down, layer, gate, w_down_bf16=None):
    b, n, d = x.shape
    hidden = w_gate.shape[2]
    tm = _tile(n, 1024, _SUBLANES_BF16)
    u = _matmul(h, w_gate, b2=w_up, b_sel=layer, tm=tm, tn=_tile(hidden, 256, _LANES), out_dtype=_BF16,
                side_cast=None if w_down_bf16 is not None else (w_down, layer), name="ffn_gate_up")
    if w_down_bf16 is None:
        u, w_down_bf16 = u
    out = _matmul(u, w_down_bf16[None], tm=_tile(n, 512, _SUBLANES_BF16), tn=_tile(d, 512, _LANES),
                  out_dtype=_F32, resid=x, gate=gate, name="ffn_down")
    return out, w_down_bf16


def kernel(x, c, ctx, c_ctx, w_ada, b_ada, g_mix, g_ffn, fourier_w_out, mla_w_a, mla_g_q, mla_g_kv, mla_w_uq,
           mla_w_ukv, mla_w_o, w_gate, w_up, w_down, g_final):
    b, n, d = x.shape
    depth = w_ada.shape[0]
    cond = jnp.concatenate([c, c_ctx[None]], axis=0)
    cond = jnp.pad(cond, ((0, -(b + 1) % 8), (0, 0)))
    mod = _ada_modulation(cond, w_ada, b_ada)
    dg = d // FOURIER_GROUPS
    tables = _fourier_tables(dg)

    xc = ctx
    for i in range(depth):
        need_ctx = i < depth - 1
        j = i // 2
        lat = [mod[i, :b, s * d:(s + 1) * d].reshape(b, 1, d) for s in range(N_MOD)]
        con = [mod[i, b:b + 1, s * d:(s + 1) * d].reshape(1, 1, d) for s in range(N_MOD)]
        h = _rms_norm(x, g_mix[i], lat[1], lat[0], out_dtype=_BF16)
        hc = _rms_norm(xc, g_mix[i], con[1], con[0], out_dtype=_BF16)
        tm = _tile(n, 1024, _SUBLANES_BF16)
        if i % 2 == 0:
            f, w_out = _fourier_2d(h, tables, side_cast=(fourier_w_out, j))
            w_out = w_out[None]
            x = _matmul(f, w_out, tm=tm, tn=_tile(d, 512, _LANES), out_dtype=_F32, resid=x, gate=lat[2],
                        name="fourier_out_proj")
            if need_ctx:
                xc = _matmul(_fourier_2d(hc, tables), w_out, tm=_tile(xc.shape[1], 1024, _SUBLANES_BF16),
                             tn=_tile(d, 512, _LANES), out_dtype=_F32, resid=xc, gate=con[2],
                             name="fourier_out_proj_ctx")
        else:
            assert not need_ctx, "context queries are only implemented for the Fourier mixer layers"
            x = _mla_mix(h, hc, mla_w_a[j], mla_g_q[j], mla_g_kv[j], mla_w_uq[j], mla_w_ukv[j], mla_w_o, j,
                         x, lat[2])
        x, w_down_i = _ffn(x, _rms_norm(x, g_ffn[i], lat[4], lat[3], out_dtype=_BF16), w_gate, w_up, w_down, i,
                           lat[5])
        if need_ctx:
            hc2 = _rms_norm(xc, g_ffn[i], con[4], con[3], out_dtype=_BF16)
            nc = xc.shape[1]
            xc, _ = _ffn(xc.reshape(1, b * nc, d), hc2.reshape(1, b * nc, d), w_gate, w_up, w_down, i, con[5],
                         w_down_bf16=w_down_i)
            xc = xc.reshape(b, nc, d)
    return _rms_norm(x, g_final, out_dtype=x.dtype)
```

```python
import functools
import math

import jax
import jax.numpy as jnp
import numpy as np
from jax import lax
from jax.experimental import pallas as pl
from jax.experimental.pallas import tpu as pltpu

_F32 = jnp.float32
_BF16 = jnp.bfloat16

GRID_W = 64
FOURIER_GROUPS = 4
MLA_HEADS = 64
Q_LORA = 1536
KV_LORA = 512
QK_NOPE = 128
QK_ROPE = 64
V_HEAD = 128
ROPE_BASE = 10000.0
EPS = 1e-6
N_MOD = 6

_LANES = 128
_SUBLANES_BF16 = 16
_VMEM_CAP_BYTES = 56 * 1024 * 1024
_HEAD_SLOT = 2 * _LANES
_ATTN_KEY_CHUNK = 512
_ATTN_LOOKAHEAD = 2
_LOG2E = math.log2(math.e)


def _tile(dim, pref, align):
    if dim <= pref:
        return dim
    t = (pref // align) * align
    while t >= align:
        if dim % t == 0:
            return t
        t -= align
    return dim


def _nbytes(shape, dtype):
    return int(np.prod(shape)) * jnp.dtype(dtype).itemsize


def _params(semantics, vmem_bytes):
    return pltpu.CompilerParams(dimension_semantics=semantics,
                                vmem_limit_bytes=int(min(_VMEM_CAP_BYTES, vmem_bytes)))


def _ada_body(c_ref, w_ref, b_ref, o_ref):
    c = c_ref[...]
    s = c * jax.nn.sigmoid(c)
    acc = jnp.dot(s.astype(_BF16), w_ref[...].astype(_BF16), preferred_element_type=_F32)
    o_ref[...] = acc + b_ref[...]


def _ada_modulation(cc, w_ada, b_ada):
    depth, d, n = w_ada.shape
    r = cc.shape[0]
    tn = _tile(n, 512, _LANES)
    vmem = 2 * _nbytes((d, tn), _F32) + _nbytes((d, tn), _BF16) + 8 * _nbytes((r, tn), _F32) + (4 << 20)
    return pl.pallas_call(
        _ada_body,
        grid=(depth, n // tn),
        in_specs=[
            pl.BlockSpec((r, d), lambda l, j: (0, 0)),
            pl.BlockSpec((None, d, tn), lambda l, j: (l, 0, j)),
            pl.BlockSpec((None, 1, tn), lambda l, j: (l, 0, j)),
        ],
        out_specs=pl.BlockSpec((None, r, tn), lambda l, j: (l, 0, j)),
        out_shape=jax.ShapeDtypeStruct((depth, r, n), _F32),
        compiler_params=_params(("parallel", "parallel"), vmem),
        name="ada_modulation",
    )(cc, w_ada, b_ada.reshape(depth, 1, n))


def _norm_body(*refs, modulate):
    if modulate:
        x_ref, g_ref, sc_ref, sh_ref, o_ref = refs
    else:
        x_ref, g_ref, o_ref = refs
    x = x_ref[...]
    y = x * lax.rsqrt(jnp.mean(x * x, axis=-1, keepdims=True) + EPS) * g_ref[...]
    if modulate:
        y = y * (1.0 + sc_ref[...]) + sh_ref[...]
    o_ref[...] = y.astype(o_ref.dtype)


def _rms_norm(x, g, scale=None, shift=None, *, out_dtype):
    b, n, d = x.shape
    tr = _tile(n, 256, 8)
    modulate = scale is not None
    in_specs = [pl.BlockSpec((None, tr, d), lambda bb, i: (bb, i, 0)),
                pl.BlockSpec((1, d), lambda bb, i: (0, 0))]
    args = [x, g.reshape(1, d)]
    if modulate:
        per_batch = scale.shape[0] > 1
        mod_spec = pl.BlockSpec((None, 1, d), (lambda bb, i: (bb, 0, 0)) if per_batch else (lambda bb, i: (0, 0, 0)))
        in_specs += [mod_spec, mod_spec]
        args += [scale, shift]
    vmem = 2 * _nbytes((tr, d), _F32) + 2 * _nbytes((tr, d), out_dtype) + 3 * _nbytes((tr, d), _F32) + (2 << 20)
    return pl.pallas_call(
        functools.partial(_norm_body, modulate=modulate),
        grid=(b, n // tr),
        in_specs=in_specs,
        out_specs=pl.BlockSpec((None, tr, d), lambda bb, i: (bb, i, 0)),
        out_shape=jax.ShapeDtypeStruct((b, n, d), out_dtype),
        compiler_params=_params(("parallel", "parallel"), vmem),
        name="rms_norm_modulate" if modulate else "rms_norm",
    )(*args)


def _rope_rotate(t, cos, sin, upper_half):
    lane = lax.broadcasted_iota(jnp.int32, t.shape, 1)
    quarter = QK_ROPE // 4
    even_chunk = (lane // quarter) % 2 == 0
    swapped = jnp.where(even_chunk, pltpu.roll(t, _LANES - quarter, 1), pltpu.roll(t, quarter, 1))
    valid = (lane >= QK_ROPE) if upper_half else (lane < QK_ROPE)
    return jnp.where(valid, t * cos + swapped * sin, 0.0)


def _q_head_slots(acc, cos, sin, q_scale):
    tiles = [acc[:, i * _LANES:(i + 1) * _LANES] for i in range(acc.shape[1] // _LANES)]
    low_half = lax.broadcasted_iota(jnp.int32, tiles[0].shape, 1) < QK_ROPE
    cos_lo, cos_hi = cos[:, :_LANES], cos[:, _LANES:]
    sin_lo, sin_hi = sin[:, :_LANES], sin[:, _LANES:]
    parts = []
    for p in range(len(tiles) // 3):
        t0, t1, t2 = tiles[3 * p:3 * p + 3]
        parts += [t0 * q_scale, _rope_rotate(t1, cos_lo, sin_lo, False) * q_scale,
                  jnp.where(low_half, t2, t1) * q_scale, _rope_rotate(t2, cos_hi, sin_hi, True) * q_scale]
    return jnp.concatenate(parts, axis=1)


def _side_cast_blocks(rows, cols, steps):
    widths = [c for c in range(_LANES, cols + 1, _LANES) if cols % c == 0]
    for bc in sorted(widths, key=lambda c: abs(c - 4 * _LANES)):
        ncb = cols // bc
        if steps % ncb == 0 and rows % (steps // ncb) == 0 and (rows // (steps // ncb)) % _SUBLANES_BF16 == 0:
            return rows // (steps // ncb), bc
    raise ValueError(f"no tile-aligned split of {(rows, cols)} into {steps} blocks")


def _mm_body(*refs, n_b, epilogue, nk, q_scale, nt, side):
    a_ref = refs[0]
    b_refs = refs[1:1 + n_b]
    pos = 1 + n_b
    extra = ()
    if epilogue in ("resgate", "qrope"):
        extra = refs[pos:pos + 2]
        pos += 2
    elif epilogue == "mirror":
        extra = refs[pos:pos + 2]
        pos += 2
    if side:
        refs[pos + 2][...] = refs[pos][...].astype(refs[pos + 2].dtype)
        refs = refs[:pos] + refs[pos + 1:pos + 2] + refs[pos + 3:]
    o_ref = refs[pos]
    acc_refs = refs[pos + 1:]

    def finish(accs):
        if epilogue == "plain":
            out = accs[0]
        elif epilogue == "swiglu":
            g, u = accs
            out = g * jax.nn.sigmoid(g) * u
        elif epilogue == "resgate":
            res_ref, gate_ref = extra
            out = res_ref[...] + gate_ref[...] * accs[0]
        elif epilogue == "qrope":
            cos_ref, sin_ref = extra
            out = _q_head_slots(accs[0], cos_ref[...], sin_ref[...], q_scale)
        elif epilogue == "mirror":
            p, neg_q = accs
            e_ref, perm_ref = extra
            plus = p - neg_q
            first = plus[:, :_LANES]
            lane = lax.broadcasted_iota(jnp.int32, first.shape, 1)
            first = jnp.where(lane == 0, e_ref[...].astype(_F32), first)
            plus = jnp.concatenate([first, plus[:, _LANES:]], axis=1).astype(_BF16)
            upper = jnp.dot(plus, perm_ref[...], preferred_element_type=_F32)
            out = jnp.concatenate([p + neg_q, upper], axis=1)
        o_ref[...] = out.astype(o_ref.dtype)

    def dots():
        a = a_ref[...]
        outs = []
        for b_ref in b_refs:
            b = b_ref[...].astype(_BF16)
            if nt:
                outs.append(lax.dot_general(a, b, (((1,), (1,)), ((), ())), preferred_element_type=_F32))
            else:
                outs.append(jnp.dot(a, b, preferred_element_type=_F32))
        return outs

    if nk == 1:
        finish(dots())
        return

    k = pl.program_id(3)

    if epilogue == "mirror":
        half = nk // 2
        for idx, first_k in ((0, 0), (1, half)):
            @pl.when(k == first_k)
            def _(idx=idx):
                acc_refs[idx][...] = dots()[0]

            @pl.when((k > first_k) & (k < first_k + half))
            def _(idx=idx):
                acc_refs[idx][...] += dots()[0]
    else:
        @pl.when(k == 0)
        def _():
            for acc_ref, d in zip(acc_refs, dots()):
                acc_ref[...] = d

        @pl.when(k > 0)
        def _():
            for acc_ref, d in zip(acc_refs, dots()):
                acc_ref[...] += d

    @pl.when(k == nk - 1)
    def _():
        finish([acc_ref[...] for acc_ref in acc_refs])


def _matmul(a, b, *, tm, tn, tk=None, out_dtype, b2=None, b_sel=None, nt=False, resid=None, gate=None,
            rope=None, q_scale=1.0, side_cast=None, mirror=None, k_extent=None, name):
    ba, m, kdim = a.shape
    kdim = kdim if k_extent is None else k_extent
    lb = b.shape[0]
    n = b.shape[1] if nt else b.shape[2]
    b_batched = b_sel is None and lb > 1
    bt = max(ba, lb if b_batched else 1, resid.shape[0] if resid is not None else 1)
    tk = kdim if tk is None else tk
    nk = kdim // tk
    assert m % tm == 0 and n % tn == 0 and kdim % tk == 0, (name, a.shape, b.shape, tm, tn, tk)

    def bidx(count):
        return (lambda z: z) if count > 1 else (lambda z: 0)

    ai = bidx(ba)
    if b_sel is not None:
        bi = lambda z: b_sel
    else:
        bi = bidx(lb)
    if nt:
        b_spec = pl.BlockSpec((None, tn, tk), lambda z, i, j, k: (bi(z), j, k))
        b_block = (tn, tk)
    else:
        b_spec = pl.BlockSpec((None, tk, tn), lambda z, i, j, k: (bi(z), k, j))
        b_block = (tk, tn)
    in_specs = [pl.BlockSpec((None, tm, tk), lambda z, i, j, k: (ai(z), i, k)), b_spec]
    args = [a, b]
    n_b = 1
    epilogue = "plain"
    blocks = [((tm, tk), a.dtype), (b_block, b.dtype)]
    if b2 is not None:
        in_specs.append(b_spec)
        args.append(b2)
        blocks.append((b_block, b2.dtype))
        n_b = 2
        epilogue = "swiglu"
    if resid is not None:
        gi = bidx(gate.shape[0])
        in_specs += [pl.BlockSpec((None, tm, tn), lambda z, i, j, k: (z, i, j)),
                     pl.BlockSpec((None, 1, tn), lambda z, i, j, k: (gi(z), 0, j))]
        args += [resid, gate]
        blocks.append(((tm, tn), resid.dtype))
        epilogue = "resgate"
    out_tn, out_n = tn, n
    if rope is not None:
        head = QK_NOPE + QK_ROPE
        assert tn % (2 * head) == 0 and nk == 1
        out_tn, out_n = tn // head * _HEAD_SLOT, n // head * _HEAD_SLOT
        table_spec = pl.BlockSpec((tm, _HEAD_SLOT), lambda z, i, j, k: (i, 0))
        in_specs += [table_spec, table_spec]
        args += list(rope)
        blocks += [((tm, _HEAD_SLOT), _F32)] * 2
        epilogue = "qrope"
    n_acc = n_b
    if mirror is not None:
        assert nk % 2 == 0 and tn % _LANES == 0
        out_tn, out_n, n_acc = 2 * tn, 2 * n, 2
        unmirror = np.zeros((tn, tn), np.float32)
        unmirror[0, 0] = 1.0
        unmirror[np.arange(1, tn), tn - np.arange(1, tn)] = 1.0
        in_specs += [pl.BlockSpec((None, tm, _LANES), lambda z, i, j, k: (z, i, j)),
                     pl.BlockSpec((tn, tn), lambda z, i, j, k: (0, 0))]
        args += [mirror, jnp.asarray(unmirror, _BF16)]
        blocks += [((tm, _LANES), mirror.dtype), ((tn, tn), _BF16)]
        epilogue = "mirror"
    blocks.append(((tm, out_tn), out_dtype))
    out_specs = pl.BlockSpec((None, tm, out_tn), lambda z, i, j, k: (z, i, j))
    out_shape = jax.ShapeDtypeStruct((bt, m, out_n), out_dtype)
    if side_cast is not None:
        src, sel = side_cast
        steps_i, steps_j = m // tm, n // tn
        br, bc = _side_cast_blocks(src.shape[1], src.shape[2], bt * steps_i * steps_j)
        ncb = src.shape[2] // bc

        def step(z, i, j):
            return (z * steps_i + i) * steps_j + j

        in_specs.append(pl.BlockSpec((None, br, bc), lambda z, i, j, k: (sel, step(z, i, j) // ncb, step(z, i, j) % ncb)))
        args.append(src)
        out_specs = [out_specs, pl.BlockSpec((br, bc), lambda z, i, j, k: (step(z, i, j) // ncb, step(z, i, j) % ncb))]
        out_shape = [out_shape, jax.ShapeDtypeStruct(src.shape[1:], _BF16)]
        blocks += [((br, bc), src.dtype), ((br, bc), _BF16)]
    scratch = [pltpu.VMEM((tm, tn), _F32) for _ in range(n_acc)] if nk > 1 else []
    cast_tmp = n_b * _nbytes(b_block, _BF16) if b.dtype != _BF16 else 0
    vmem = (2 * sum(_nbytes(s, dt) for s, dt in blocks) + cast_tmp + (n_b + 2) * _nbytes((tm, tn), _F32)
            + (2 << 20))
    return pl.pallas_call(
        functools.partial(_mm_body, n_b=n_b, epilogue=epilogue, nk=nk, q_scale=q_scale, nt=nt,
                          side=side_cast is not None),
        grid=(bt, m // tm, n // tn, nk),
        in_specs=in_specs,
        out_specs=out_specs,
        out_shape=out_shape,
        scratch_shapes=scratch,
        compiler_params=_params(("parallel", "parallel", "parallel", "arbitrary"), vmem),
        name=name,
    )(*args)


def _dft_cos_sin(n, scale):
    idx = np.arange(n)
    ang = ((idx[:, None] * idx[None, :]) % n) * (2.0 * np.pi / n)
    return np.cos(ang) * scale, np.sin(ang) * scale


def _dft_gen_body(ca_ref, sa_ref, cb_ref, sb_ref, o_ref, *, n, scale):
    ca, sa, cb, sb = ca_ref[...], sa_ref[...], cb_ref[...], sb_ref[...]
    o_ref[:, :n] = ((ca * cb - sa * sb) * scale).astype(o_ref.dtype)
    o_ref[:, n:] = ((sa * cb + ca * sb) * -scale).astype(o_ref.dtype)


def _position_dft_matrix(n):
    tr = _tile(n, 256, _SUBLANES_BF16)
    t = jnp.arange(n, dtype=jnp.int32)[None, :]
    two_pi_n = 2.0 * np.pi / n
    beta = ((jnp.arange(tr, dtype=jnp.int32)[:, None] * t) % n).astype(_F32) * two_pi_n
    alpha = ((jnp.arange(0, n, tr, dtype=jnp.int32)[:, None] * t) % n).astype(_F32) * two_pi_n
    alpha = alpha.reshape(n // tr, 1, n)
    row_spec = pl.BlockSpec((None, 1, n), lambda i: (i, 0, 0))
    base_spec = pl.BlockSpec((tr, n), lambda i: (0, 0))
    vmem = 4 * _nbytes((tr, n), _F32) + 2 * _nbytes((tr, 2 * n), _BF16) + 4 * _nbytes((tr, n), _F32) + (2 << 20)
    return pl.pallas_call(
        functools.partial(_dft_gen_body, n=n, scale=float(n ** -0.5)),
        grid=(n // tr,),
        in_specs=[row_spec, row_spec, base_spec, base_spec],
        out_specs=pl.BlockSpec((tr, 2 * n), lambda i: (i, 0)),
        out_shape=jax.ShapeDtypeStruct((n, 2 * n), _BF16),
        compiler_params=_params(("parallel",), vmem),
        name="fourier_position_matrix",
    )(jnp.cos(alpha), jnp.sin(alpha), jnp.cos(beta), jnp.sin(beta))


def _channel_dft_body(h_ref, cs_ref, mid_w_ref, z_ref, mid_ref):
    h = h_ref[...]
    z_ref[...] = jnp.dot(h, cs_ref[...], preferred_element_type=_F32).astype(z_ref.dtype)

    @pl.when(pl.program_id(3) == 0)
    def _():
        mid_ref[...] = jnp.dot(h, mid_w_ref[...], preferred_element_type=_F32).astype(mid_ref.dtype)


def _channel_dft_half(h, cs_half, mid_w):
    b, n, d = h.shape
    dg, dh = cs_half.shape[1:]
    groups = d // dg
    tm = _tile(n, 1024, _SUBLANES_BF16)
    vmem = (2 * (_nbytes((tm, dg), _BF16) + _nbytes((dg, dh), _BF16) + _nbytes((dg, _LANES), _BF16)
                 + _nbytes((tm, dh), _BF16) + _nbytes((tm, _LANES), _BF16)) + 3 * _nbytes((tm, dh), _F32) + (2 << 20))
    return pl.pallas_call(
        _channel_dft_body,
        grid=(b, n // tm, groups, 2),
        in_specs=[pl.BlockSpec((None, tm, dg), lambda z, i, g, c: (z, i, g)),
                  pl.BlockSpec((None, dg, dh), lambda z, i, g, c: (c, 0, 0)),
                  pl.BlockSpec((dg, _LANES), lambda z, i, g, c: (0, 0))],
        out_specs=[pl.BlockSpec((None, None, tm, dh), lambda z, i, g, c: (z, c, i, g)),
                   pl.BlockSpec((None, tm, _LANES), lambda z, i, g, c: (z, i, g))],
        out_shape=[jax.ShapeDtypeStruct((b, 2, n, groups * dh), _BF16),
                   jax.ShapeDtypeStruct((b, n, groups * _LANES), _BF16)],
        compiler_params=_params(("parallel", "parallel", "parallel", "arbitrary"), vmem),
        name="fourier_channel_dft",
    )(h, cs_half, mid_w)


def _fourier_tables(dg):
    cc, sc = _dft_cos_sin(dg, dg ** -0.5)
    half = dg // 2
    mid_w = np.pad(cc[:, half:half + 1], ((0, 0), (0, _LANES - 1)))
    return jnp.asarray(np.stack([cc[:, :half], sc[:, :half]]), _BF16), jnp.asarray(mid_w, _BF16)


def _fourier_2d(h, tables, side_cast=None):
    b, n, d = h.shape
    cs_half, mid_w = tables
    z, mid = _channel_dft_half(h, cs_half, mid_w)
    a = _position_dft_matrix(n)[None]
    tm = _tile(n, 1024, _SUBLANES_BF16)
    e = _matmul(a, mid, tm=tm, tn=mid.shape[2], tk=_tile(n, 2048, _LANES), k_extent=n, out_dtype=_BF16,
                name="fourier_position_dft_mid")
    return _matmul(a, z.reshape(b, 2 * n, d // 2), tm=tm, tn=cs_half.shape[2], tk=_tile(n, 4096, _LANES),
                   out_dtype=_BF16, mirror=e, side_cast=side_cast, name="fourier_position_dft")


def _mla_prep_body(*refs, with_q, with_rope):
    refs = list(refs)
    a_ref = refs.pop(0)
    gq_ref = refs.pop(0) if with_q else None
    gkv_ref = refs.pop(0)
    cos_ref, sin_ref = (refs.pop(0), refs.pop(0)) if with_rope else (None, None)
    cq_ref = refs.pop(0) if with_q else None
    ckv_ref, kr_ref = refs

    def norm(t, g):
        return t * lax.rsqrt(jnp.mean(t * t, axis=-1, keepdims=True) + EPS) * g

    off = 0
    if with_q:
        cq_ref[...] = norm(a_ref[:, :Q_LORA], gq_ref[...]).astype(cq_ref.dtype)
        off = Q_LORA
    ckv_ref[...] = norm(a_ref[:, off:off + KV_LORA], gkv_ref[...]).astype(ckv_ref.dtype)
    kr = a_ref[:, off + KV_LORA:off + KV_LORA + _LANES]
    if with_rope:
        kr = _rope_rotate(kr, cos_ref[:, :_LANES], sin_ref[:, :_LANES], False)
    kr_ref[:, :_LANES] = kr.astype(kr_ref.dtype)
    kr_ref[:, _LANES:] = pltpu.roll(kr, QK_ROPE, 1).astype(kr_ref.dtype)


def _mla_prep(a, g_q, g_kv, rope):
    b, n, w = a.shape
    with_q = g_q is not None
    with_rope = rope is not None
    tr = _tile(n, 256, _SUBLANES_BF16)
    row = lambda z, i: (z, i, 0)
    in_specs = [pl.BlockSpec((None, tr, w), row)]
    args = [a]
    if with_q:
        in_specs.append(pl.BlockSpec((1, Q_LORA), lambda z, i: (0, 0)))
        args.append(g_q.reshape(1, Q_LORA))
    in_specs.append(pl.BlockSpec((1, KV_LORA), lambda z, i: (0, 0)))
    args.append(g_kv.reshape(1, KV_LORA))
    if with_rope:
        in_specs += [pl.BlockSpec((tr, _HEAD_SLOT), lambda z, i: (i, 0))] * 2
        args += list(rope)
    out_specs, out_shape = [], []
    if with_q:
        out_specs.append(pl.BlockSpec((None, tr, Q_LORA), row))
        out_shape.append(jax.ShapeDtypeStruct((b, n, Q_LORA), _BF16))
    out_specs += [pl.BlockSpec((None, tr, KV_LORA), row), pl.BlockSpec((None, tr, _HEAD_SLOT), row)]
    out_shape += [jax.ShapeDtypeStruct((b, n, KV_LORA), _BF16), jax.ShapeDtypeStruct((b, n, _HEAD_SLOT), _BF16)]
    vmem = 6 * _nbytes((tr, w), _F32) + (2 << 20)
    return pl.pallas_call(
        functools.partial(_mla_prep_body, with_q=with_q, with_rope=with_rope),
        grid=(b, n // tr),
        in_specs=in_specs,
        out_specs=out_specs,
        out_shape=out_shape,
        compiler_params=_params(("parallel", "parallel"), vmem),
        name="mla_prep_latent" if with_q else "mla_prep_context",
    )(*args)


def _attn_body(q_ref, k_ref, kr_ref, vt_ref, o_ref, kcat_ref, vext_ref, *, chunks):
    @pl.when(pl.program_id(2) == 0)
    def _():
        kcat_ref[:, :QK_NOPE] = k_ref[...]
        kcat_ref[:, QK_NOPE:] = kr_ref[...]
        vext_ref[:V_HEAD, :] = vt_ref[...]
        vext_ref[V_HEAD:, :] = jnp.ones((_SUBLANES_BF16, vext_ref.shape[1]), vext_ref.dtype)

    q = q_ref[...]

    def scores(start, size):
        return lax.dot_general(kcat_ref[start:start + size, :], q, (((1,), (1,)), ((), ())),
                               preferred_element_type=_F32)

    starts = [sum(chunks[:c]) for c in range(len(chunks))]
    m = acc = None
    ahead = [scores(starts[c], chunks[c]) for c in range(min(_ATTN_LOOKAHEAD, len(chunks)))]
    for c, (start, size) in enumerate(zip(starts, chunks)):
        st = ahead.pop(0)
        nxt = c + _ATTN_LOOKAHEAD
        if nxt < len(chunks):
            ahead.append(scores(starts[nxt], chunks[nxt]))
        mc = jnp.max(st, axis=0, keepdims=True).astype(_BF16)
        m_new = mc if m is None else jnp.maximum(m, mc)
        p = jnp.exp2(st.astype(_BF16) - m_new)
        pv = jnp.dot(vext_ref[:, start:start + size], p, preferred_element_type=_F32)
        if m is None:
            acc = pv
        else:
            acc = jnp.exp2(m.astype(_F32) - m_new.astype(_F32)) * acc + pv
        m = m_new
    o_ref[...] = (acc[:V_HEAD] / acc[V_HEAD:V_HEAD + 1]).T.astype(o_ref.dtype)


def _attention(q, k, kr, vt):
    b, n, _ = q.shape
    nk = k.shape[1]
    tq = _tile(n, 4096, _LANES)
    chunk = _ATTN_KEY_CHUNK
    chunks = [chunk] * (nk // chunk) + ([nk % chunk] if nk % chunk else [])
    vmem = (4 * _nbytes((tq, _HEAD_SLOT), _BF16) + 5 * _nbytes((nk, _LANES), _BF16)
            + 2 * _nbytes((V_HEAD, nk), _BF16) + 6 * _nbytes((min(chunk, nk), tq), _F32) + (4 << 20))
    return pl.pallas_call(
        functools.partial(_attn_body, chunks=chunks),
        grid=(b, MLA_HEADS, n // tq),
        in_specs=[pl.BlockSpec((None, tq, _HEAD_SLOT), lambda z, h, i: (z, i, h)),
                  pl.BlockSpec((None, nk, QK_NOPE), lambda z, h, i: (z, 0, h)),
                  pl.BlockSpec((None, nk, _LANES), lambda z, h, i: (z, 0, h % 2)),
                  pl.BlockSpec((None, V_HEAD, nk), lambda z, h, i: (z, h, 0))],
        out_specs=pl.BlockSpec((None, tq, V_HEAD), lambda z, h, i: (z, i, h)),
        out_shape=jax.ShapeDtypeStruct((b, n, MLA_HEADS * V_HEAD), _BF16),
        scratch_shapes=[pltpu.VMEM((nk, _HEAD_SLOT), _BF16), pltpu.VMEM((V_HEAD + _SUBLANES_BF16, nk), _BF16)],
        compiler_params=_params(("parallel", "parallel", "arbitrary"), vmem),
        name="mla_attention",
    )(q, k, kr, vt)


def _rope_tables(n):
    rows_n = n // GRID_W
    row = jnp.repeat(jnp.arange(rows_n, dtype=_F32), GRID_W)
    col = jnp.tile(jnp.arange(GRID_W, dtype=_F32), rows_n)
    n_freq = QK_ROPE // 4
    inv_freq = ROPE_BASE ** (-jnp.arange(n_freq, dtype=_F32) / n_freq)
    ar, ac = row[:, None] * inv_freq, col[:, None] * inv_freq
    zeros = jnp.zeros((n, _LANES - QK_ROPE), _F32)
    cos = jnp.concatenate([jnp.cos(ar), jnp.cos(ar), jnp.cos(ac), jnp.cos(ac)], axis=1)
    sin = jnp.concatenate([-jnp.sin(ar), jnp.sin(ar), -jnp.sin(ac), jnp.sin(ac)], axis=1)
    return (jnp.concatenate([cos, zeros, zeros, cos], axis=1), jnp.concatenate([sin, zeros, zeros, sin], axis=1))


def _mla_mix(h, hc, w_a, g_q, g_kv, w_uq, w_ukv, w_o, w_o_sel, x, gate):
    b, n, d = h.shape
    nc = hc.shape[1]
    rope = _rope_tables(n)
    a_width = Q_LORA + KV_LORA + QK_ROPE
    a_pad = -(-a_width // _HEAD_SLOT) * _HEAD_SLOT
    w_a_pad = jnp.pad(w_a, ((0, 0), (0, a_pad - a_width))).astype(_BF16)[None]
    tm = _tile(n, 1024, _SUBLANES_BF16)
    a_l = _matmul(h, w_a_pad, tm=tm, tn=_tile(a_pad, 3 * _HEAD_SLOT, _HEAD_SLOT), out_dtype=_F32,
                  name="mla_down_latent")
    a_c = _matmul(hc, w_a_pad[:, :, Q_LORA:], tm=_tile(nc, 1024, _SUBLANES_BF16), tn=_HEAD_SLOT,
                  out_dtype=_F32, name="mla_down_context")
    cq, ckv_l, kr_l = _mla_prep(a_l, g_q, g_kv, rope)
    ckv_c, kr_c = _mla_prep(a_c, None, g_kv, None)
    ckv = jnp.concatenate([ckv_c, ckv_l], axis=1)
    kr = jnp.concatenate([kr_c, kr_l], axis=1)

    head = QK_NOPE + QK_ROPE
    q, w_o_bf16 = _matmul(cq, w_uq[None], tm=tm, tn=_tile(MLA_HEADS * head, 4 * head, 2 * head), out_dtype=_BF16,
                          rope=rope, q_scale=float(head ** -0.5 * _LOG2E), side_cast=(w_o, w_o_sel),
                          name="mla_q_up")
    nkv = nc + n
    w_ukv_h = w_ukv.reshape(KV_LORA, MLA_HEADS // 2, 2, QK_NOPE + V_HEAD)
    w_uk_h = w_ukv_h[..., :QK_NOPE]
    w_uk_h = jnp.stack([w_uk_h[:, :, 0], jnp.roll(w_uk_h[:, :, 1], -(QK_NOPE // 2), axis=-1)], axis=2)
    w_uk = w_uk_h.reshape(KV_LORA, MLA_HEADS * QK_NOPE).astype(_BF16)[None]
    w_ukv_h = w_ukv_h.reshape(KV_LORA, MLA_HEADS, QK_NOPE + V_HEAD)
    w_uv_t = w_ukv_h[:, :, QK_NOPE:].reshape(KV_LORA, MLA_HEADS * V_HEAD).T.astype(_BF16)[None]
    k = _matmul(ckv, w_uk, tm=_tile(nkv, 1088, _SUBLANES_BF16), tn=_tile(MLA_HEADS * QK_NOPE, 2048, _LANES),
                out_dtype=_BF16, name="mla_k_up")
    vt = _matmul(w_uv_t, ckv, nt=True, tm=_tile(MLA_HEADS * V_HEAD, 1024, _SUBLANES_BF16),
                 tn=_tile(nkv, 2176, _LANES), out_dtype=_BF16, name="mla_v_up_transposed")
    o = _attention(q, k, kr, vt)
    return _matmul(o, w_o_bf16[None], tm=_tile(n, 512, _SUBLANES_BF16), tn=_tile(d, 512, _LANES),
                   out_dtype=_F32, resid=x, gate=gate, name="mla_out_proj")


def _ffn(x, h, w_gate, w_up, w_down, layer, gate, w_down_bf16=None):
    b, n, d = x.shape
    hidden = w_gate.shape[2]
    tm = _tile(n, 1024, _SUBLANES_BF16)
    u = _matmul(h, w_gate, b2=w_up, b_sel=layer, tm=tm, tn=_tile(hidden, 256, _LANES), out_dtype=_BF16,
                side_cast=None if w_down_bf16 is not None else (w_down, layer), name="ffn_gate_up")
    if w_down_bf16 is None:
        u, w_down_bf16 = u
    out = _matmul(u, w_down_bf16[None], tm=_tile(n, 512, _SUBLANES_BF16), tn=_tile(d, 512, _LANES),
                  out_dtype=_F32, resid=x, gate=gate, name="ffn_down")
    return out, w_down_bf16


def kernel(x, c, ctx, c_ctx, w_ada, b_ada, g_mix, g_ffn, fourier_w_out, mla_w_a, mla_g_q, mla_g_kv, mla_w_uq,
           mla_w_ukv, mla_w_o, w_gate, w_up, w_down, g_final):
    b, n, d = x.shape
    depth = w_ada.shape[0]
    cond = jnp.concatenate([c, c_ctx[None]], axis=0)
    cond = jnp.pad(cond, ((0, -(b + 1) % 8), (0, 0)))
    mod = _ada_modulation(cond, w_ada, b_ada)
    dg = d // FOURIER_GROUPS
    tables = _fourier_tables(dg)

    xc = ctx
    for i in range(depth):
        need_ctx = i < depth - 1
        j = i // 2
        lat = [mod[i, :b, s * d:(s + 1) * d].reshape(b, 1, d) for s in range(N_MOD)]
        con = [mod[i, b:b + 1, s * d:(s + 1) * d].reshape(1, 1, d) for s in range(N_MOD)]
        h = _rms_norm(x, g_mix[i], lat[1], lat[0], out_dtype=_BF16)
        hc = _rms_norm(xc, g_mix[i], con[1], con[0], out_dtype=_BF16)
        tm = _tile(n, 1024, _SUBLANES_BF16)
        if i % 2 == 0:
            f, w_out = _fourier_2d(h, tables, side_cast=(fourier_w_out, j))
            w_out = w_out[None]
            x = _matmul(f, w_out, tm=tm, tn=_tile(d, 512, _LANES), out_dtype=_F32, resid=x, gate=lat[2],
                        name="fourier_out_proj")
            if need_ctx:
                xc = _matmul(_fourier_2d(hc, tables), w_out, tm=_tile(xc.shape[1], 1024, _SUBLANES_BF16),
                             tn=_tile(d, 512, _LANES), out_dtype=_F32, resid=xc, gate=con[2],
                             name="fourier_out_proj_ctx")
        else:
            assert not need_ctx, "context queries are only implemented for the Fourier mixer layers"
            x = _mla_mix(h, hc, mla_w_a[j], mla_g_q[j], mla_g_kv[j], mla_w_uq[j], mla_w_ukv[j], mla_w_o, j,
                         x, lat[2])
        x, w_down_i = _ffn(x, _rms_norm(x, g_ffn[i], lat[4], lat[3], out_dtype=_BF16), w_gate, w_up, w_down, i,
                           lat[5])
        if need_ctx:
            hc2 = _rms_norm(xc, g_ffn[i], con[4], con[3], out_dtype=_BF16)
            nc = xc.shape[1]
            xc, _ = _ffn(xc.reshape(1, b * nc, d), hc2.reshape(1, b * nc, d), w_gate, w_up, w_down, i, con[5],
                         w_down_bf16=w_down_i)
            xc = xc.reshape(b, nc, d)
    return _rms_norm(x, g_final, out_dtype=x.dtype)
```

```python
import functools
import math

import jax
import jax.numpy as jnp
import numpy as np
from jax import lax
from jax.experimental import pallas as pl
from jax.experimental.pallas import tpu as pltpu

_F32 = jnp.float32
_BF16 = jnp.bfloat16

GRID_W = 64
FOURIER_GROUPS = 4
MLA_HEADS = 64
Q_LORA = 1536
KV_LORA = 512
QK_NOPE = 128
QK_ROPE = 64
V_HEAD = 128
ROPE_BASE = 10000.0
EPS = 1e-6
N_MOD = 6

_LANES = 128
_SUBLANES_BF16 = 16
_VMEM_CAP_BYTES = 56 * 1024 * 1024
_HEAD_SLOT = 2 * _LANES
_ATTN_KEY_CHUNK = 512
_ATTN_QUERY_SLAB = 256
_ATTN_LOOKAHEAD = 4
_LOG2E = math.log2(math.e)


def _tile(dim, pref, align):
    if dim <= pref:
        return dim
    t = (pref // align) * align
    while t >= align:
        if dim % t == 0:
            return t
        t -= align
    return dim


def _nbytes(shape, dtype):
    return int(np.prod(shape)) * jnp.dtype(dtype).itemsize


def _params(semantics, vmem_bytes):
    return pltpu.CompilerParams(dimension_semantics=semantics,
                                vmem_limit_bytes=int(min(_VMEM_CAP_BYTES, vmem_bytes)))


def _ada_body(c_ref, w_ref, b_ref, o_ref):
    c = c_ref[...]
    s = c * jax.nn.sigmoid(c)
    acc = jnp.dot(s.astype(_BF16), w_ref[...].astype(_BF16), preferred_element_type=_F32)
    o_ref[...] = acc + b_ref[...]


def _ada_modulation(cc, w_ada, b_ada):
    depth, d, n = w_ada.shape
    r = cc.shape[0]
    tn = _tile(n, 512, _LANES)
    vmem = 2 * _nbytes((d, tn), _F32) + _nbytes((d, tn), _BF16) + 8 * _nbytes((r, tn), _F32) + (4 << 20)
    return pl.pallas_call(
        _ada_body,
        grid=(depth, n // tn),
        in_specs=[
            pl.BlockSpec((r, d), lambda l, j: (0, 0)),
            pl.BlockSpec((None, d, tn), lambda l, j: (l, 0, j)),
            pl.BlockSpec((None, 1, tn), lambda l, j: (l, 0, j)),
        ],
        out_specs=pl.BlockSpec((None, r, tn), lambda l, j: (l, 0, j)),
        out_shape=jax.ShapeDtypeStruct((depth, r, n), _F32),
        compiler_params=_params(("parallel", "parallel"), vmem),
        name="ada_modulation",
    )(cc, w_ada, b_ada.reshape(depth, 1, n))


def _norm_body(*refs, modulate):
    if modulate:
        x_ref, g_ref, sc_ref, sh_ref, o_ref = refs
    else:
        x_ref, g_ref, o_ref = refs
    x = x_ref[...]
    y = x * lax.rsqrt(jnp.mean(x * x, axis=-1, keepdims=True) + EPS) * g_ref[...]
    if modulate:
        y = y * (1.0 + sc_ref[...]) + sh_ref[...]
    o_ref[...] = y.astype(o_ref.dtype)


def _rms_norm(x, g, scale=None, shift=None, *, out_dtype):
    b, n, d = x.shape
    tr = _tile(n, 256, 8)
    modulate = scale is not None
    in_specs = [pl.BlockSpec((None, tr, d), lambda bb, i: (bb, i, 0)),
                pl.BlockSpec((1, d), lambda bb, i: (0, 0))]
    args = [x, g.reshape(1, d)]
    if modulate:
        per_batch = scale.shape[0] > 1
        mod_spec = pl.BlockSpec((None, 1, d), (lambda bb, i: (bb, 0, 0)) if per_batch else (lambda bb, i: (0, 0, 0)))
        in_specs += [mod_spec, mod_spec]
        args += [scale, shift]
    vmem = 2 * _nbytes((tr, d), _F32) + 2 * _nbytes((tr, d), out_dtype) + 3 * _nbytes((tr, d), _F32) + (2 << 20)
    return pl.pallas_call(
        functools.partial(_norm_body, modulate=modulate),
        grid=(b, n // tr),
        in_specs=in_specs,
        out_specs=pl.BlockSpec((None, tr, d), lambda bb, i: (bb, i, 0)),
        out_shape=jax.ShapeDtypeStruct((b, n, d), out_dtype),
        compiler_params=_params(("parallel", "parallel"), vmem),
        name="rms_norm_modulate" if modulate else "rms_norm",
    )(*args)


def _rope_rotate(t, cos, sin, upper_half):
    lane = lax.broadcasted_iota(jnp.int32, t.shape, 1)
    quarter = QK_ROPE // 4
    even_chunk = (lane // quarter) % 2 == 0
    swapped = jnp.where(even_chunk, pltpu.roll(t, _LANES - quarter, 1), pltpu.roll(t, quarter, 1))
    valid = (lane >= QK_ROPE) if upper_half else (lane < QK_ROPE)
    return jnp.where(valid, t * cos + swapped * sin, 0.0)


def _q_head_slots(acc, cos, sin, q_scale):
    tiles = [acc[:, i * _LANES:(i + 1) * _LANES] for i in range(acc.shape[1] // _LANES)]
    low_half = lax.broadcasted_iota(jnp.int32, tiles[0].shape, 1) < QK_ROPE
    cos_lo, cos_hi = cos[:, :_LANES], cos[:, _LANES:]
    sin_lo, sin_hi = sin[:, :_LANES], sin[:, _LANES:]
    parts = []
    for p in range(len(tiles) // 3):
        t0, t1, t2 = tiles[3 * p:3 * p + 3]
        parts += [t0 * q_scale, _rope_rotate(t1, cos_lo, sin_lo, False) * q_scale,
                  jnp.where(low_half, t2, t1) * q_scale, _rope_rotate(t2, cos_hi, sin_hi, True) * q_scale]
    return jnp.concatenate(parts, axis=1)


def _side_cast_blocks(rows, cols, steps):
    widths = [c for c in range(_LANES, cols + 1, _LANES) if cols % c == 0]
    for bc in sorted(widths, key=lambda c: abs(c - 4 * _LANES)):
        ncb = cols // bc
        if steps % ncb == 0 and rows % (steps // ncb) == 0 and (rows // (steps // ncb)) % _SUBLANES_BF16 == 0:
            return rows // (steps // ncb), bc
    raise ValueError(f"no tile-aligned split of {(rows, cols)} into {steps} blocks")


def _mm_body(*refs, n_b, epilogue, nk, q_scale, nt, side):
    a_ref = refs[0]
    b_refs = refs[1:1 + n_b]
    pos = 1 + n_b
    extra = ()
    if epilogue in ("resgate", "qrope"):
        extra = refs[pos:pos + 2]
        pos += 2
    elif epilogue == "mirror":
        extra = refs[pos:pos + 2]
        pos += 2
    if side:
        refs[pos + 2][...] = refs[pos][...].astype(refs[pos + 2].dtype)
        refs = refs[:pos] + refs[pos + 1:pos + 2] + refs[pos + 3:]
    o_ref = refs[pos]
    acc_refs = refs[pos + 1:]

    def finish(accs):
        if epilogue == "plain":
            out = accs[0]
        elif epilogue == "swiglu":
            g, u = accs
            out = g * jax.nn.sigmoid(g) * u
        elif epilogue == "resgate":
            res_ref, gate_ref = extra
            out = res_ref[...] + gate_ref[...] * accs[0]
        elif epilogue == "qrope":
            cos_ref, sin_ref = extra
            out = _q_head_slots(accs[0], cos_ref[...], sin_ref[...], q_scale)
        elif epilogue == "mirror":
            p, neg_q = accs
            e_ref, perm_ref = extra
            plus = p - neg_q
            first = plus[:, :_LANES]
            lane = lax.broadcasted_iota(jnp.int32, first.shape, 1)
            first = jnp.where(lane == 0, e_ref[...].astype(_F32), first)
            plus = jnp.concatenate([first, plus[:, _LANES:]], axis=1).astype(_BF16)
            upper = jnp.dot(plus, perm_ref[...], preferred_element_type=_F32)
            out = jnp.concatenate([p + neg_q, upper], axis=1)
        o_ref[...] = out.astype(o_ref.dtype)

    def dots():
        a = a_ref[...]
        outs = []
        for b_ref in b_refs:
            b = b_ref[...].astype(_BF16)
            if nt:
                outs.append(lax.dot_general(a, b, (((1,), (1,)), ((), ())), preferred_element_type=_F32))
            else:
                outs.append(jnp.dot(a, b, preferred_element_type=_F32))
        return outs

    if nk == 1:
        finish(dots())
        return

    k = pl.program_id(3)

    if epilogue == "mirror":
        half = nk // 2
        for idx, first_k in ((0, 0), (1, half)):
            @pl.when(k == first_k)
            def _(idx=idx):
                acc_refs[idx][...] = dots()[0]

            @pl.when((k > first_k) & (k < first_k + half))
            def _(idx=idx):
                acc_refs[idx][...] += dots()[0]
    else:
        @pl.when(k == 0)
        def _():
            for acc_ref, d in zip(acc_refs, dots()):
                acc_ref[...] = d

        @pl.when(k > 0)
        def _():
            for acc_ref, d in zip(acc_refs, dots()):
                acc_ref[...] += d

    @pl.when(k == nk - 1)
    def _():
        finish([acc_ref[...] for acc_ref in acc_refs])


def _matmul(a, b, *, tm, tn, tk=None, out_dtype, b2=None, b_sel=None, nt=False, resid=None, gate=None,
            rope=None, q_scale=1.0, side_cast=None, mirror=None, k_extent=None, name):
    ba, m, kdim = a.shape
    kdim = kdim if k_extent is None else k_extent
    lb = b.shape[0]
    n = b.shape[1] if nt else b.shape[2]
    b_batched = b_sel is None and lb > 1
    bt = max(ba, lb if b_batched else 1, resid.shape[0] if resid is not None else 1)
    tk = kdim if tk is None else tk
    nk = kdim // tk
    assert m % tm == 0 and n % tn == 0 and kdim % tk == 0, (name, a.shape, b.shape, tm, tn, tk)

    def bidx(count):
        return (lambda z: z) if count > 1 else (lambda z: 0)

    ai = bidx(ba)
    if b_sel is not None:
        bi = lambda z: b_sel
    else:
        bi = bidx(lb)
    if nt:
        b_spec = pl.BlockSpec((None, tn, tk), lambda z, i, j, k: (bi(z), j, k))
        b_block = (tn, tk)
    else:
        b_spec = pl.BlockSpec((None, tk, tn), lambda z, i, j, k: (bi(z), k, j))
        b_block = (tk, tn)
    in_specs = [pl.BlockSpec((None, tm, tk), lambda z, i, j, k: (ai(z), i, k)), b_spec]
    args = [a, b]
    n_b = 1
    epilogue = "plain"
    blocks = [((tm, tk), a.dtype), (b_block, b.dtype)]
    if b2 is not None:
        in_specs.append(b_spec)
        args.append(b2)
        blocks.append((b_block, b2.dtype))
        n_b = 2
        epilogue = "swiglu"
    if resid is not None:
        gi = bidx(gate.shape[0])
        in_specs += [pl.BlockSpec((None, tm, tn), lambda z, i, j, k: (z, i, j)),
                     pl.BlockSpec((None, 1, tn), lambda z, i, j, k: (gi(z), 0, j))]
        args += [resid, gate]
        blocks.append(((tm, tn), resid.dtype))
        epilogue = "resgate"
    out_tn, out_n = tn, n
    if rope is not None:
        head = QK_NOPE + QK_ROPE
        assert tn % (2 * head) == 0 and nk == 1
        out_tn, out_n = tn // head * _HEAD_SLOT, n // head * _HEAD_SLOT
        table_spec = pl.BlockSpec((tm, _HEAD_SLOT), lambda z, i, j, k: (i, 0))
        in_specs += [table_spec, table_spec]
        args += list(rope)
        blocks += [((tm, _HEAD_SLOT), _F32)] * 2
        epilogue = "qrope"
    n_acc = n_b
    if mirror is not None:
        assert nk % 2 == 0 and tn % _LANES == 0
        out_tn, out_n, n_acc = 2 * tn, 2 * n, 2
        unmirror = np.zeros((tn, tn), np.float32)
        unmirror[0, 0] = 1.0
        unmirror[np.arange(1, tn), tn - np.arange(1, tn)] = 1.0
        in_specs += [pl.BlockSpec((None, tm, _LANES), lambda z, i, j, k: (z, i, j)),
                     pl.BlockSpec((tn, tn), lambda z, i, j, k: (0, 0))]
        args += [mirror, jnp.asarray(unmirror, _BF16)]
        blocks += [((tm, _LANES), mirror.dtype), ((tn, tn), _BF16)]
        epilogue = "mirror"
    blocks.append(((tm, out_tn), out_dtype))
    out_specs = pl.BlockSpec((None, tm, out_tn), lambda z, i, j, k: (z, i, j))
    out_shape = jax.ShapeDtypeStruct((bt, m, out_n), out_dtype)
    if side_cast is not None:
        src, sel = side_cast
        steps_i, steps_j = m // tm, n // tn
        br, bc = _side_cast_blocks(src.shape[1], src.shape[2], bt * steps_i * steps_j)
        ncb = src.shape[2] // bc

        def step(z, i, j):
            return (z * steps_i + i) * steps_j + j

        in_specs.append(pl.BlockSpec((None, br, bc), lambda z, i, j, k: (sel, step(z, i, j) // ncb, step(z, i, j) % ncb)))
        args.append(src)
        out_specs = [out_specs, pl.BlockSpec((br, bc), lambda z, i, j, k: (step(z, i, j) // ncb, step(z, i, j) % ncb))]
        out_shape = [out_shape, jax.ShapeDtypeStruct(src.shape[1:], _BF16)]
        blocks += [((br, bc), src.dtype), ((br, bc), _BF16)]
    scratch = [pltpu.VMEM((tm, tn), _F32) for _ in range(n_acc)] if nk > 1 else []
    cast_tmp = n_b * _nbytes(b_block, _BF16) if b.dtype != _BF16 else 0
    vmem = (2 * sum(_nbytes(s, dt) for s, dt in blocks) + cast_tmp + (n_b + 2) * _nbytes((tm, tn), _F32)
            + (2 << 20))
    return pl.pallas_call(
        functools.partial(_mm_body, n_b=n_b, epilogue=epilogue, nk=nk, q_scale=q_scale, nt=nt,
                          side=side_cast is not None),
        grid=(bt, m // tm, n // tn, nk),
        in_specs=in_specs,
        out_specs=out_specs,
        out_shape=out_shape,
        scratch_shapes=scratch,
        compiler_params=_params(("parallel", "parallel", "parallel", "arbitrary"), vmem),
        name=name,
    )(*args)


def _dft_cos_sin(n, scale):
    idx = np.arange(n)
    ang = ((idx[:, None] * idx[None, :]) % n) * (2.0 * np.pi / n)
    return np.cos(ang) * scale, np.sin(ang) * scale


def _dft_gen_body(ca_ref, sa_ref, cb_ref, sb_ref, o_ref, *, n, scale):
    ca, sa, cb, sb = ca_ref[...], sa_ref[...], cb_ref[...], sb_ref[...]
    o_ref[:, :n] = ((ca * cb - sa * sb) * scale).astype(o_ref.dtype)
    o_ref[:, n:] = ((sa * cb + ca * sb) * -scale).astype(o_ref.dtype)


def _position_dft_matrix(n):
    tr = _tile(n, 256, _SUBLANES_BF16)
    t = jnp.arange(n, dtype=jnp.int32)[None, :]
    two_pi_n = 2.0 * np.pi / n
    beta = ((jnp.arange(tr, dtype=jnp.int32)[:, None] * t) % n).astype(_F32) * two_pi_n
    alpha = ((jnp.arange(0, n, tr, dtype=jnp.int32)[:, None] * t) % n).astype(_F32) * two_pi_n
    alpha = alpha.reshape(n // tr, 1, n)
    row_spec = pl.BlockSpec((None, 1, n), lambda i: (i, 0, 0))
    base_spec = pl.BlockSpec((tr, n), lambda i: (0, 0))
    vmem = 4 * _nbytes((tr, n), _F32) + 2 * _nbytes((tr, 2 * n), _BF16) + 4 * _nbytes((tr, n), _F32) + (2 << 20)
    return pl.pallas_call(
        functools.partial(_dft_gen_body, n=n, scale=float(n ** -0.5)),
        grid=(n // tr,),
        in_specs=[row_spec, row_spec, base_spec, base_spec],
        out_specs=pl.BlockSpec((tr, 2 * n), lambda i: (i, 0)),
        out_shape=jax.ShapeDtypeStruct((n, 2 * n), _BF16),
        compiler_params=_params(("parallel",), vmem),
        name="fourier_position_matrix",
    )(jnp.cos(alpha), jnp.sin(alpha), jnp.cos(beta), jnp.sin(beta))


def _channel_dft_body(h_ref, cs_ref, mid_w_ref, z_ref, mid_ref):
    h = h_ref[...]
    z_ref[...] = jnp.dot(h, cs_ref[...], preferred_element_type=_F32).astype(z_ref.dtype)

    @pl.when(pl.program_id(3) == 0)
    def _():
        mid_ref[...] = jnp.dot(h, mid_w_ref[...], preferred_element_type=_F32).astype(mid_ref.dtype)


def _channel_dft_half(h, cs_half, mid_w):
    b, n, d = h.shape
    dg, dh = cs_half.shape[1:]
    groups = d // dg
    tm = _tile(n, 1024, _SUBLANES_BF16)
    vmem = (2 * (_nbytes((tm, dg), _BF16) + _nbytes((dg, dh), _BF16) + _nbytes((dg, _LANES), _BF16)
                 + _nbytes((tm, dh), _BF16) + _nbytes((tm, _LANES), _BF16)) + 3 * _nbytes((tm, dh), _F32) + (2 << 20))
    return pl.pallas_call(
        _channel_dft_body,
        grid=(b, n // tm, groups, 2),
        in_specs=[pl.BlockSpec((None, tm, dg), lambda z, i, g, c: (z, i, g)),
                  pl.BlockSpec((None, dg, dh), lambda z, i, g, c: (c, 0, 0)),
                  pl.BlockSpec((dg, _LANES), lambda z, i, g, c: (0, 0))],
        out_specs=[pl.BlockSpec((None, None, tm, dh), lambda z, i, g, c: (z, c, i, g)),
                   pl.BlockSpec((None, tm, _LANES), lambda z, i, g, c: (z, i, g))],
        out_shape=[jax.ShapeDtypeStruct((b, 2, n, groups * dh), _BF16),
                   jax.ShapeDtypeStruct((b, n, groups * _LANES), _BF16)],
        compiler_params=_params(("parallel", "parallel", "parallel", "arbitrary"), vmem),
        name="fourier_channel_dft",
    )(h, cs_half, mid_w)


def _fourier_tables(dg):
    cc, sc = _dft_cos_sin(dg, dg ** -0.5)
    half = dg // 2
    mid_w = np.pad(cc[:, half:half + 1], ((0, 0), (0, _LANES - 1)))
    return jnp.asarray(np.stack([cc[:, :half], sc[:, :half]]), _BF16), jnp.asarray(mid_w, _BF16)


def _fourier_2d(h, tables, side_cast=None):
    b, n, d = h.shape
    cs_half, mid_w = tables
    z, mid = _channel_dft_half(h, cs_half, mid_w)
    a = _position_dft_matrix(n)[None]
    tm = _tile(n, 1024, _SUBLANES_BF16)
    e = _matmul(a, mid, tm=tm, tn=mid.shape[2], tk=_tile(n, 2048, _LANES), k_extent=n, out_dtype=_BF16,
                name="fourier_position_dft_mid")
    return _matmul(a, z.reshape(b, 2 * n, d // 2), tm=tm, tn=cs_half.shape[2], tk=_tile(n, 4096, _LANES),
                   out_dtype=_BF16, mirror=e, side_cast=side_cast, name="fourier_position_dft")


def _mla_prep_body(*refs, with_q, with_rope):
    refs = list(refs)
    a_ref = refs.pop(0)
    gq_ref = refs.pop(0) if with_q else None
    gkv_ref = refs.pop(0)
    cos_ref, sin_ref = (refs.pop(0), refs.pop(0)) if with_rope else (None, None)
    cq_ref = refs.pop(0) if with_q else None
    ckv_ref, kr_ref = refs

    def norm(t, g):
        return t * lax.rsqrt(jnp.mean(t * t, axis=-1, keepdims=True) + EPS) * g

    off = 0
    if with_q:
        cq_ref[...] = norm(a_ref[:, :Q_LORA], gq_ref[...]).astype(cq_ref.dtype)
        off = Q_LORA
    ckv_ref[...] = norm(a_ref[:, off:off + KV_LORA], gkv_ref[...]).astype(ckv_ref.dtype)
    kr = a_ref[:, off + KV_LORA:off + KV_LORA + _LANES]
    if with_rope:
        kr = _rope_rotate(kr, cos_ref[:, :_LANES], sin_ref[:, :_LANES], False)
    kr_ref[:, :_LANES] = kr.astype(kr_ref.dtype)
    kr_ref[:, _LANES:] = pltpu.roll(kr, QK_ROPE, 1).astype(kr_ref.dtype)


def _mla_prep(a, g_q, g_kv, rope):
    b, n, w = a.shape
    with_q = g_q is not None
    with_rope = rope is not None
    tr = _tile(n, 256, _SUBLANES_BF16)
    row = lambda z, i: (z, i, 0)
    in_specs = [pl.BlockSpec((None, tr, w), row)]
    args = [a]
    if with_q:
        in_specs.append(pl.BlockSpec((1, Q_LORA), lambda z, i: (0, 0)))
        args.append(g_q.reshape(1, Q_LORA))
    in_specs.append(pl.BlockSpec((1, KV_LORA), lambda z, i: (0, 0)))
    args.append(g_kv.reshape(1, KV_LORA))
    if with_rope:
        in_specs += [pl.BlockSpec((tr, _HEAD_SLOT), lambda z, i: (i, 0))] * 2
        args += list(rope)
    out_specs, out_shape = [], []
    if with_q:
        out_specs.append(pl.BlockSpec((None, tr, Q_LORA), row))
        out_shape.append(jax.ShapeDtypeStruct((b, n, Q_LORA), _BF16))
    out_specs += [pl.BlockSpec((None, tr, KV_LORA), row), pl.BlockSpec((None, tr, _HEAD_SLOT), row)]
    out_shape += [jax.ShapeDtypeStruct((b, n, KV_LORA), _BF16), jax.ShapeDtypeStruct((b, n, _HEAD_SLOT), _BF16)]
    vmem = 6 * _nbytes((tr, w), _F32) + (2 << 20)
    return pl.pallas_call(
        functools.partial(_mla_prep_body, with_q=with_q, with_rope=with_rope),
        grid=(b, n // tr),
        in_specs=in_specs,
        out_specs=out_specs,
        out_shape=out_shape,
        compiler_params=_params(("parallel", "parallel"), vmem),
        name="mla_prep_latent" if with_q else "mla_prep_context",
    )(*args)


def _attn_body(q_ref, k_ref, kr_ref, vt_ref, o_ref, kcat_ref, vext_ref, *, chunks):
    @pl.when(pl.program_id(2) == 0)
    def _():
        kcat_ref[:, :QK_NOPE] = k_ref[...]
        kcat_ref[:, QK_NOPE:] = kr_ref[...]
        vext_ref[:V_HEAD, :] = vt_ref[...]
        vext_ref[V_HEAD:, :] = jnp.ones((_SUBLANES_BF16, vext_ref.shape[1]), vext_ref.dtype)

    starts = [sum(chunks[:c]) for c in range(len(chunks))]
    slab = min(_ATTN_QUERY_SLAB, q_ref.shape[0])
    n_slabs = q_ref.shape[0] // slab
    items = [(c, j) for c in range(len(chunks)) for j in range(n_slabs)]

    def scores(c, j):
        return lax.dot_general(kcat_ref[starts[c]:starts[c] + chunks[c], :], q_ref[j * slab:(j + 1) * slab, :],
                               (((1,), (1,)), ((), ())), preferred_element_type=_F32)

    m, acc = [None] * n_slabs, [None] * n_slabs
    ahead = [scores(*items[i]) for i in range(min(_ATTN_LOOKAHEAD, len(items)))]
    for idx, (c, j) in enumerate(items):
        st = ahead.pop(0)
        if idx + _ATTN_LOOKAHEAD < len(items):
            ahead.append(scores(*items[idx + _ATTN_LOOKAHEAD]))
        mc = jnp.max(st, axis=0, keepdims=True).astype(_BF16)
        m_new = mc if c == 0 else jnp.maximum(m[j], mc)
        p = jnp.exp2(st.astype(_BF16) - m_new)
        pv = jnp.dot(vext_ref[:, starts[c]:starts[c] + chunks[c]], p, preferred_element_type=_F32)
        acc[j] = pv if c == 0 else jnp.exp2(m[j].astype(_F32) - m_new.astype(_F32)) * acc[j] + pv
        m[j] = m_new
    for j in range(n_slabs):
        o_ref[j * slab:(j + 1) * slab, :] = (acc[j][:V_HEAD] / acc[j][V_HEAD:V_HEAD + 1]).T.astype(o_ref.dtype)


def _attention(q, k, kr, vt):
    b, n, _ = q.shape
    nk = k.shape[1]
    tq = _tile(n, 4096, _LANES)
    chunk = _ATTN_KEY_CHUNK
    chunks = [chunk] * (nk // chunk) + ([nk % chunk] if nk % chunk else [])
    vmem = (4 * _nbytes((tq, _HEAD_SLOT), _BF16) + 5 * _nbytes((nk, _LANES), _BF16)
            + 2 * _nbytes((V_HEAD, nk), _BF16) + 6 * _nbytes((min(chunk, nk), tq), _F32) + (4 << 20))
    return pl.pallas_call(
        functools.partial(_attn_body, chunks=chunks),
        grid=(b, MLA_HEADS, n // tq),
        in_specs=[pl.BlockSpec((None, tq, _HEAD_SLOT), lambda z, h, i: (z, i, h)),
                  pl.BlockSpec((None, nk, QK_NOPE), lambda z, h, i: (z, 0, h)),
                  pl.BlockSpec((None, nk, _LANES), lambda z, h, i: (z, 0, h % 2)),
                  pl.BlockSpec((None, V_HEAD, nk), lambda z, h, i: (z, h, 0))],
        out_specs=pl.BlockSpec((None, tq, V_HEAD), lambda z, h, i: (z, i, h)),
        out_shape=jax.ShapeDtypeStruct((b, n, MLA_HEADS * V_HEAD), _BF16),
        scratch_shapes=[pltpu.VMEM((nk, _HEAD_SLOT), _BF16), pltpu.VMEM((V_HEAD + _SUBLANES_BF16, nk), _BF16)],
        compiler_params=_params(("parallel", "parallel", "arbitrary"), vmem),
        name="mla_attention",
    )(q, k, kr, vt)


def _rope_tables(n):
    rows_n = n // GRID_W
    row = jnp.repeat(jnp.arange(rows_n, dtype=_F32), GRID_W)
    col = jnp.tile(jnp.arange(GRID_W, dtype=_F32), rows_n)
    n_freq = QK_ROPE // 4
    inv_freq = ROPE_BASE ** (-jnp.arange(n_freq, dtype=_F32) / n_freq)
    ar, ac = row[:, None] * inv_freq, col[:, None] * inv_freq
    zeros = jnp.zeros((n, _LANES - QK_ROPE), _F32)
    cos = jnp.concatenate([jnp.cos(ar), jnp.cos(ar), jnp.cos(ac), jnp.cos(ac)], axis=1)
    sin = jnp.concatenate([-jnp.sin(ar), jnp.sin(ar), -jnp.sin(ac), jnp.sin(ac)], axis=1)
    return (jnp.concatenate([cos, zeros, zeros, cos], axis=1), jnp.concatenate([sin, zeros, zeros, sin], axis=1))


def _mla_mix(h, hc, w_a, g_q, g_kv, w_uq, w_ukv, w_o, w_o_sel, x, gate):
    b, n, d = h.shape
    nc = hc.shape[1]
    rope = _rope_tables(n)
    a_width = Q_LORA + KV_LORA + QK_ROPE
    a_pad = -(-a_width // _HEAD_SLOT) * _HEAD_SLOT
    w_a_pad = jnp.pad(w_a, ((0, 0), (0, a_pad - a_width))).astype(_BF16)[None]
    tm = _tile(n, 1024, _SUBLANES_BF16)
    a_l = _matmul(h, w_a_pad, tm=tm, tn=_tile(a_pad, 3 * _HEAD_SLOT, _HEAD_SLOT), out_dtype=_F32,
                  name="mla_down_latent")
    a_c = _matmul(hc, w_a_pad[:, :, Q_LORA:], tm=_tile(nc, 1024, _SUBLANES_BF16), tn=_HEAD_SLOT,
                  out_dtype=_F32, name="mla_down_context")
    cq, ckv_l, kr_l = _mla_prep(a_l, g_q, g_kv, rope)
    ckv_c, kr_c = _mla_prep(a_c, None, g_kv, None)
    ckv = jnp.concatenate([ckv_c, ckv_l], axis=1)
    kr = jnp.concatenate([kr_c, kr_l], axis=1)

    head = QK_NOPE + QK_ROPE
    q, w_o_bf16 = _matmul(cq, w_uq[None], tm=tm, tn=_tile(MLA_HEADS * head, 4 * head, 2 * head), out_dtype=_BF16,
                          rope=rope, q_scale=float(head ** -0.5 * _LOG2E), side_cast=(w_o, w_o_sel),
                          name="mla_q_up")
    nkv = nc + n
    w_ukv_h = w_ukv.reshape(KV_LORA, MLA_HEADS // 2, 2, QK_NOPE + V_HEAD)
    w_uk_h = w_ukv_h[..., :QK_NOPE]
    w_uk_h = jnp.stack([w_uk_h[:, :, 0], jnp.roll(w_uk_h[:, :, 1], -(QK_NOPE // 2), axis=-1)], axis=2)
    w_uk = w_uk_h.reshape(KV_LORA, MLA_HEADS * QK_NOPE).astype(_BF16)[None]
    w_ukv_h = w_ukv_h.reshape(KV_LORA, MLA_HEADS, QK_NOPE + V_HEAD)
    w_uv_t = w_ukv_h[:, :, QK_NOPE:].reshape(KV_LORA, MLA_HEADS * V_HEAD).T.astype(_BF16)[None]
    k = _matmul(ckv, w_uk, tm=_tile(nkv, 1088, _SUBLANES_BF16), tn=_tile(MLA_HEADS * QK_NOPE, 2048, _LANES),
                out_dtype=_BF16, name="mla_k_up")
    vt = _matmul(w_uv_t, ckv, nt=True, tm=_tile(MLA_HEADS * V_HEAD, 1024, _SUBLANES_BF16),
                 tn=_tile(nkv, 2176, _LANES), out_dtype=_BF16, name="mla_v_up_transposed")
    o = _attention(q, k, kr, vt)
    return _matmul(o, w_o_bf16[None], tm=_tile(n, 512, _SUBLANES_BF16), tn=_tile(d, 512, _LANES),
                   out_dtype=_F32, resid=x, gate=gate, name="mla_out_proj")


def _ffn(x, h, w_gate, w_up, w_down, layer, gate, w_down_bf16=None):
    b, n, d = x.shape
    hidden = w_gate.shape[2]
    tm = _tile(n, 1024, _SUBLANES_BF16)
    u = _matmul(h, w_gate, b2=w_up, b_sel=layer, tm=tm, tn=_tile(hidden, 256, _LANES), out_dtype=_BF16,
                side_cast=None if w_down_bf16 is not None else (w_down, layer), name="ffn_gate_up")
    if w_down_bf16 is None:
        u, w_down_bf16 = u
    out = _matmul(u, w_down_bf16[None], tm=_tile(n, 512, _SUBLANES_BF16), tn=_tile(d, 512, _LANES),
                  out_dtype=_F32, resid=x, gate=gate, name="ffn_down")
    return out, w_down_bf16


def kernel(x, c, ctx, c_ctx, w_ada, b_ada, g_mix, g_ffn, fourier_w_out, mla_w_a, mla_g_q, mla_g_kv, mla_w_uq,
           mla_w_ukv, mla_w_o, w_gate, w_up, w_down, g_final):
    b, n, d = x.shape
    depth = w_ada.shape[0]
    cond = jnp.concatenate([c, c_ctx[None]], axis=0)
    cond = jnp.pad(cond, ((0, -(b + 1) % 8), (0, 0)))
    mod = _ada_modulation(cond, w_ada, b_ada)
    dg = d // FOURIER_GROUPS
    tables = _fourier_tables(dg)

    xc = ctx
    for i in range(depth):
        need_ctx = i < depth - 1
        j = i // 2
        lat = [mod[i, :b, s * d:(s + 1) * d].reshape(b, 1, d) for s in range(N_MOD)]
        con = [mod[i, b:b + 1, s * d:(s + 1) * d].reshape(1, 1, d) for s in range(N_MOD)]
        h = _rms_norm(x, g_mix[i], lat[1], lat[0], out_dtype=_BF16)
        hc = _rms_norm(xc, g_mix[i], con[1], con[0], out_dtype=_BF16)
        tm = _tile(n, 1024, _SUBLANES_BF16)
        if i % 2 == 0:
            f, w_out = _fourier_2d(h, tables, side_cast=(fourier_w_out, j))
            w_out = w_out[None]
            x = _matmul(f, w_out, tm=tm, tn=_tile(d, 512, _LANES), out_dtype=_F32, resid=x, gate=lat[2],
                        name="fourier_out_proj")
            if need_ctx:
                xc = _matmul(_fourier_2d(hc, tables), w_out, tm=_tile(xc.shape[1], 1024, _SUBLANES_BF16),
                             tn=_tile(d, 512, _LANES), out_dtype=_F32, resid=xc, gate=con[2],
                             name="fourier_out_proj_ctx")
        else:
            assert not need_ctx, "context queries are only implemented for the Fourier mixer layers"
            x = _mla_mix(h, hc, mla_w_a[j], mla_g_q[j], mla_g_kv[j], mla_w_uq[j], mla_w_ukv[j], mla_w_o, j,
                         x, lat[2])
        x, w_down_i = _ffn(x, _rms_norm(x, g_ffn[i], lat[4], lat[3], out_dtype=_BF16), w_gate, w_up, w_down, i,
                           lat[5])
        if need_ctx:
            hc2 = _rms_norm(xc, g_ffn[i], con[4], con[3], out_dtype=_BF16)
            nc = xc.shape[1]
            xc, _ = _ffn(xc.reshape(1, b * nc, d), hc2.reshape(1, b * nc, d), w_gate, w_up, w_down, i, con[5],
                         w_down_bf16=w_down_i)
            xc = xc.reshape(b, nc, d)
    return _rms_norm(x, g_final, out_dtype=x.dtype)
```

```python
import functools
import math

import jax
import jax.numpy as jnp
import numpy as np
from jax import lax
from jax.experimental import pallas as pl
from jax.experimental.pallas import tpu as pltpu

_F32 = jnp.float32
_BF16 = jnp.bfloat16

GRID_W = 64
FOURIER_GROUPS = 4
MLA_HEADS = 64
Q_LORA = 1536
KV_LORA = 512
QK_NOPE = 128
QK_ROPE = 64
V_HEAD = 128
ROPE_BASE = 10000.0
EPS = 1e-6
N_MOD = 6

_LANES = 128
_SUBLANES_BF16 = 16
_VMEM_CAP_BYTES = 56 * 1024 * 1024
_HEAD_SLOT = 2 * _LANES
_MM_ROW_SLAB = 256
_ATTN_KEY_CHUNK = 512
_ATTN_QUERY_SLAB = 256
_ATTN_LOOKAHEAD = 4
_LOG2E = math.log2(math.e)


def _tile(dim, pref, align):
    if dim <= pref:
        return dim
    t = (pref // align) * align
    while t >= align:
        if dim % t == 0:
            return t
        t -= align
    return dim


def _nbytes(shape, dtype):
    return int(np.prod(shape)) * jnp.dtype(dtype).itemsize


def _params(semantics, vmem_bytes):
    return pltpu.CompilerParams(dimension_semantics=semantics,
                                vmem_limit_bytes=int(min(_VMEM_CAP_BYTES, vmem_bytes)))


def _ada_body(c_ref, w_ref, b_ref, o_ref):
    c = c_ref[...]
    s = c * jax.nn.sigmoid(c)
    acc = jnp.dot(s.astype(_BF16), w_ref[...].astype(_BF16), preferred_element_type=_F32)
    o_ref[...] = acc + b_ref[...]


def _ada_modulation(cc, w_ada, b_ada):
    depth, d, n = w_ada.shape
    r = cc.shape[0]
    tn = _tile(n, 512, _LANES)
    vmem = 2 * _nbytes((d, tn), _F32) + _nbytes((d, tn), _BF16) + 8 * _nbytes((r, tn), _F32) + (4 << 20)
    return pl.pallas_call(
        _ada_body,
        grid=(depth, n // tn),
        in_specs=[
            pl.BlockSpec((r, d), lambda l, j: (0, 0)),
            pl.BlockSpec((None, d, tn), lambda l, j: (l, 0, j)),
            pl.BlockSpec((None, 1, tn), lambda l, j: (l, 0, j)),
        ],
        out_specs=pl.BlockSpec((None, r, tn), lambda l, j: (l, 0, j)),
        out_shape=jax.ShapeDtypeStruct((depth, r, n), _F32),
        compiler_params=_params(("parallel", "parallel"), vmem),
        name="ada_modulation",
    )(cc, w_ada, b_ada.reshape(depth, 1, n))


def _norm_body(*refs, modulate):
    if modulate:
        x_ref, g_ref, sc_ref, sh_ref, o_ref = refs
    else:
        x_ref, g_ref, o_ref = refs
    x = x_ref[...]
    y = x * lax.rsqrt(jnp.mean(x * x, axis=-1, keepdims=True) + EPS) * g_ref[...]
    if modulate:
        y = y * (1.0 + sc_ref[...]) + sh_ref[...]
    o_ref[...] = y.astype(o_ref.dtype)


def _rms_norm(x, g, scale=None, shift=None, *, out_dtype):
    b, n, d = x.shape
    tr = _tile(n, 256, 8)
    modulate = scale is not None
    in_specs = [pl.BlockSpec((None, tr, d), lambda bb, i: (bb, i, 0)),
                pl.BlockSpec((1, d), lambda bb, i: (0, 0))]
    args = [x, g.reshape(1, d)]
    if modulate:
        per_batch = scale.shape[0] > 1
        mod_spec = pl.BlockSpec((None, 1, d), (lambda bb, i: (bb, 0, 0)) if per_batch else (lambda bb, i: (0, 0, 0)))
        in_specs += [mod_spec, mod_spec]
        args += [scale, shift]
    vmem = 2 * _nbytes((tr, d), _F32) + 2 * _nbytes((tr, d), out_dtype) + 3 * _nbytes((tr, d), _F32) + (2 << 20)
    return pl.pallas_call(
        functools.partial(_norm_body, modulate=modulate),
        grid=(b, n // tr),
        in_specs=in_specs,
        out_specs=pl.BlockSpec((None, tr, d), lambda bb, i: (bb, i, 0)),
        out_shape=jax.ShapeDtypeStruct((b, n, d), out_dtype),
        compiler_params=_params(("parallel", "parallel"), vmem),
        name="rms_norm_modulate" if modulate else "rms_norm",
    )(*args)


def _rope_rotate(t, cos, sin, upper_half):
    lane = lax.broadcasted_iota(jnp.int32, t.shape, 1)
    quarter = QK_ROPE // 4
    even_chunk = (lane // quarter) % 2 == 0
    swapped = jnp.where(even_chunk, pltpu.roll(t, _LANES - quarter, 1), pltpu.roll(t, quarter, 1))
    valid = (lane >= QK_ROPE) if upper_half else (lane < QK_ROPE)
    return jnp.where(valid, t * cos + swapped * sin, 0.0)


def _q_head_slots(acc, cos, sin, q_scale):
    tiles = [acc[:, i * _LANES:(i + 1) * _LANES] for i in range(acc.shape[1] // _LANES)]
    low_half = lax.broadcasted_iota(jnp.int32, tiles[0].shape, 1) < QK_ROPE
    cos_lo, cos_hi = cos[:, :_LANES], cos[:, _LANES:]
    sin_lo, sin_hi = sin[:, :_LANES], sin[:, _LANES:]
    parts = []
    for p in range(len(tiles) // 3):
        t0, t1, t2 = tiles[3 * p:3 * p + 3]
        parts += [t0 * q_scale, _rope_rotate(t1, cos_lo, sin_lo, False) * q_scale,
                  jnp.where(low_half, t2, t1) * q_scale, _rope_rotate(t2, cos_hi, sin_hi, True) * q_scale]
    return jnp.concatenate(parts, axis=1)


def _side_cast_blocks(rows, cols, steps):
    widths = [c for c in range(_LANES, cols + 1, _LANES) if cols % c == 0]
    for bc in sorted(widths, key=lambda c: abs(c - 4 * _LANES)):
        ncb = cols // bc
        if steps % ncb == 0 and rows % (steps // ncb) == 0 and (rows // (steps // ncb)) % _SUBLANES_BF16 == 0:
            return rows // (steps // ncb), bc
    raise ValueError(f"no tile-aligned split of {(rows, cols)} into {steps} blocks")


def _mm_body(*refs, n_b, epilogue, nk, q_scale, nt, side):
    a_ref = refs[0]
    b_refs = refs[1:1 + n_b]
    pos = 1 + n_b
    extra = ()
    if epilogue in ("resgate", "qrope"):
        extra = refs[pos:pos + 2]
        pos += 2
    elif epilogue == "mirror":
        extra = refs[pos:pos + 2]
        pos += 2
    if side:
        refs[pos + 2][...] = refs[pos][...].astype(refs[pos + 2].dtype)
        refs = refs[:pos] + refs[pos + 1:pos + 2] + refs[pos + 3:]
    o_ref = refs[pos]
    acc_refs = refs[pos + 1:]

    def finish(accs, rows=slice(None)):
        if epilogue == "plain":
            out = accs[0]
        elif epilogue == "swiglu":
            g, u = accs
            out = g * jax.nn.sigmoid(g) * u
        elif epilogue == "resgate":
            res_ref, gate_ref = extra
            out = res_ref[rows, :] + gate_ref[...] * accs[0]
        elif epilogue == "qrope":
            cos_ref, sin_ref = extra
            out = _q_head_slots(accs[0], cos_ref[rows, :], sin_ref[rows, :], q_scale)
        elif epilogue == "mirror":
            p, neg_q = accs
            e_ref, perm_ref = extra
            plus = p - neg_q
            first = plus[:, :_LANES]
            lane = lax.broadcasted_iota(jnp.int32, first.shape, 1)
            first = jnp.where(lane == 0, e_ref[...].astype(_F32), first)
            plus = jnp.concatenate([first, plus[:, _LANES:]], axis=1).astype(_BF16)
            upper = jnp.dot(plus, perm_ref[...], preferred_element_type=_F32)
            out = jnp.concatenate([p + neg_q, upper], axis=1)
        o_ref[rows, :] = out.astype(o_ref.dtype)

    def dots(rows=slice(None), bs=None):
        a = a_ref[rows, :]
        if bs is None:
            bs = [b_ref[...].astype(_BF16) for b_ref in b_refs]
        if nt:
            return [lax.dot_general(a, b, (((1,), (1,)), ((), ())), preferred_element_type=_F32) for b in bs]
        return [jnp.dot(a, b, preferred_element_type=_F32) for b in bs]

    if nk == 1:
        tm = a_ref.shape[0]
        slab = _MM_ROW_SLAB if epilogue == "qrope" and tm % _MM_ROW_SLAB == 0 else tm
        bs = [b_ref[...].astype(_BF16) for b_ref in b_refs]
        for r in range(tm // slab):
            rows = slice(r * slab, (r + 1) * slab)
            finish(dots(rows, bs), rows)
        return

    k = pl.program_id(3)

    if epilogue == "mirror":
        half = nk // 2
        for idx, first_k in ((0, 0), (1, half)):
            @pl.when(k == first_k)
            def _(idx=idx):
                acc_refs[idx][...] = dots()[0]

            @pl.when((k > first_k) & (k < first_k + half))
            def _(idx=idx):
                acc_refs[idx][...] += dots()[0]
    else:
        @pl.when(k == 0)
        def _():
            for acc_ref, d in zip(acc_refs, dots()):
                acc_ref[...] = d

        @pl.when(k > 0)
        def _():
            for acc_ref, d in zip(acc_refs, dots()):
                acc_ref[...] += d

    @pl.when(k == nk - 1)
    def _():
        finish([acc_ref[...] for acc_ref in acc_refs])


def _matmul(a, b, *, tm, tn, tk=None, out_dtype, b2=None, b_sel=None, nt=False, resid=None, gate=None,
            rope=None, q_scale=1.0, side_cast=None, mirror=None, k_extent=None, name):
    ba, m, kdim = a.shape
    kdim = kdim if k_extent is None else k_extent
    lb = b.shape[0]
    n = b.shape[1] if nt else b.shape[2]
    b_batched = b_sel is None and lb > 1
    bt = max(ba, lb if b_batched else 1, resid.shape[0] if resid is not None else 1)
    tk = kdim if tk is None else tk
    nk = kdim // tk
    assert m % tm == 0 and n % tn == 0 and kdim % tk == 0, (name, a.shape, b.shape, tm, tn, tk)

    def bidx(count):
        return (lambda z: z) if count > 1 else (lambda z: 0)

    ai = bidx(ba)
    if b_sel is not None:
        bi = lambda z: b_sel
    else:
        bi = bidx(lb)
    if nt:
        b_spec = pl.BlockSpec((None, tn, tk), lambda z, i, j, k: (bi(z), j, k))
        b_block = (tn, tk)
    else:
        b_spec = pl.BlockSpec((None, tk, tn), lambda z, i, j, k: (bi(z), k, j))
        b_block = (tk, tn)
    in_specs = [pl.BlockSpec((None, tm, tk), lambda z, i, j, k: (ai(z), i, k)), b_spec]
    args = [a, b]
    n_b = 1
    epilogue = "plain"
    blocks = [((tm, tk), a.dtype), (b_block, b.dtype)]
    if b2 is not None:
        in_specs.append(b_spec)
        args.append(b2)
        blocks.append((b_block, b2.dtype))
        n_b = 2
        epilogue = "swiglu"
    if resid is not None:
        gi = bidx(gate.shape[0])
        in_specs += [pl.BlockSpec((None, tm, tn), lambda z, i, j, k: (z, i, j)),
                     pl.BlockSpec((None, 1, tn), lambda z, i, j, k: (gi(z), 0, j))]
        args += [resid, gate]
        blocks.append(((tm, tn), resid.dtype))
        epilogue = "resgate"
    out_tn, out_n = tn, n
    if rope is not None:
        head = QK_NOPE + QK_ROPE
        assert tn % (2 * head) == 0 and nk == 1
        out_tn, out_n = tn // head * _HEAD_SLOT, n // head * _HEAD_SLOT
        table_spec = pl.BlockSpec((tm, _HEAD_SLOT), lambda z, i, j, k: (i, 0))
        in_specs += [table_spec, table_spec]
        args += list(rope)
        blocks += [((tm, _HEAD_SLOT), _F32)] * 2
        epilogue = "qrope"
    n_acc = n_b
    if mirror is not None:
        assert nk % 2 == 0 and tn % _LANES == 0
        out_tn, out_n, n_acc = 2 * tn, 2 * n, 2
        unmirror = np.zeros((tn, tn), np.float32)
        unmirror[0, 0] = 1.0
        unmirror[np.arange(1, tn), tn - np.arange(1, tn)] = 1.0
        in_specs += [pl.BlockSpec((None, tm, _LANES), lambda z, i, j, k: (z, i, j)),
                     pl.BlockSpec((tn, tn), lambda z, i, j, k: (0, 0))]
        args += [mirror, jnp.asarray(unmirror, _BF16)]
        blocks += [((tm, _LANES), mirror.dtype), ((tn, tn), _BF16)]
        epilogue = "mirror"
    blocks.append(((tm, out_tn), out_dtype))
    out_specs = pl.BlockSpec((None, tm, out_tn), lambda z, i, j, k: (z, i, j))
    out_shape = jax.ShapeDtypeStruct((bt, m, out_n), out_dtype)
    if side_cast is not None:
        src, sel = side_cast
        steps_i, steps_j = m // tm, n // tn
        br, bc = _side_cast_blocks(src.shape[1], src.shape[2], bt * steps_i * steps_j)
        ncb = src.shape[2] // bc

        def step(z, i, j):
            return (z * steps_i + i) * steps_j + j

        in_specs.append(pl.BlockSpec((None, br, bc), lambda z, i, j, k: (sel, step(z, i, j) // ncb, step(z, i, j) % ncb)))
        args.append(src)
        out_specs = [out_specs, pl.BlockSpec((br, bc), lambda z, i, j, k: (step(z, i, j) // ncb, step(z, i, j) % ncb))]
        out_shape = [out_shape, jax.ShapeDtypeStruct(src.shape[1:], _BF16)]
        blocks += [((br, bc), src.dtype), ((br, bc), _BF16)]
    scratch = [pltpu.VMEM((tm, tn), _F32) for _ in range(n_acc)] if nk > 1 else []
    cast_tmp = n_b * _nbytes(b_block, _BF16) if b.dtype != _BF16 else 0
    vmem = (2 * sum(_nbytes(s, dt) for s, dt in blocks) + cast_tmp + (n_b + 2) * _nbytes((tm, tn), _F32)
            + (2 << 20))
    return pl.pallas_call(
        functools.partial(_mm_body, n_b=n_b, epilogue=epilogue, nk=nk, q_scale=q_scale, nt=nt,
                          side=side_cast is not None),
        grid=(bt, m // tm, n // tn, nk),
        in_specs=in_specs,
        out_specs=out_specs,
        out_shape=out_shape,
        scratch_shapes=scratch,
        compiler_params=_params(("parallel", "parallel", "parallel", "arbitrary"), vmem),
        name=name,
    )(*args)


def _dft_cos_sin(n, scale):
    idx = np.arange(n)
    ang = ((idx[:, None] * idx[None, :]) % n) * (2.0 * np.pi / n)
    return np.cos(ang) * scale, np.sin(ang) * scale


def _dft_gen_body(ca_ref, sa_ref, cb_ref, sb_ref, o_ref, *, n, scale):
    ca, sa, cb, sb = ca_ref[...], sa_ref[...], cb_ref[...], sb_ref[...]
    o_ref[:, :n] = ((ca * cb - sa * sb) * scale).astype(o_ref.dtype)
    o_ref[:, n:] = ((sa * cb + ca * sb) * -scale).astype(o_ref.dtype)


def _position_dft_matrix(n):
    tr = _tile(n, 256, _SUBLANES_BF16)
    t = jnp.arange(n, dtype=jnp.int32)[None, :]
    two_pi_n = 2.0 * np.pi / n
    beta = ((jnp.arange(tr, dtype=jnp.int32)[:, None] * t) % n).astype(_F32) * two_pi_n
    alpha = ((jnp.arange(0, n, tr, dtype=jnp.int32)[:, None] * t) % n).astype(_F32) * two_pi_n
    alpha = alpha.reshape(n // tr, 1, n)
    row_spec = pl.BlockSpec((None, 1, n), lambda i: (i, 0, 0))
    base_spec = pl.BlockSpec((tr, n), lambda i: (0, 0))
    vmem = 4 * _nbytes((tr, n), _F32) + 2 * _nbytes((tr, 2 * n), _BF16) + 4 * _nbytes((tr, n), _F32) + (2 << 20)
    return pl.pallas_call(
        functools.partial(_dft_gen_body, n=n, scale=float(n ** -0.5)),
        grid=(n // tr,),
        in_specs=[row_spec, row_spec, base_spec, base_spec],
        out_specs=pl.BlockSpec((tr, 2 * n), lambda i: (i, 0)),
        out_shape=jax.ShapeDtypeStruct((n, 2 * n), _BF16),
        compiler_params=_params(("parallel",), vmem),
        name="fourier_position_matrix",
    )(jnp.cos(alpha), jnp.sin(alpha), jnp.cos(beta), jnp.sin(beta))


def _channel_dft_body(h_ref, cs_ref, mid_w_ref, z_ref, mid_ref):
    h = h_ref[...]
    z_ref[...] = jnp.dot(h, cs_ref[...], preferred_element_type=_F32).astype(z_ref.dtype)

    @pl.when(pl.program_id(3) == 0)
    def _():
        mid_ref[...] = jnp.dot(h, mid_w_ref[...], preferred_element_type=_F32).astype(mid_ref.dtype)


def _channel_dft_half(h, cs_half, mid_w):
    b, n, d = h.shape
    dg, dh = cs_half.shape[1:]
    groups = d // dg
    tm = _tile(n, 1024, _SUBLANES_BF16)
    vmem = (2 * (_nbytes((tm, dg), _BF16) + _nbytes((dg, dh), _BF16) + _nbytes((dg, _LANES), _BF16)
                 + _nbytes((tm, dh), _BF16) + _nbytes((tm, _LANES), _BF16)) + 3 * _nbytes((tm, dh), _F32) + (2 << 20))
    return pl.pallas_call(
        _channel_dft_body,
        grid=(b, n // tm, groups, 2),
        in_specs=[pl.BlockSpec((None, tm, dg), lambda z, i, g, c: (z, i, g)),
                  pl.BlockSpec((None, dg, dh), lambda z, i, g, c: (c, 0, 0)),
                  pl.BlockSpec((dg, _LANES), lambda z, i, g, c: (0, 0))],
        out_specs=[pl.BlockSpec((None, None, tm, dh), lambda z, i, g, c: (z, c, i, g)),
                   pl.BlockSpec((None, tm, _LANES), lambda z, i, g, c: (z, i, g))],
        out_shape=[jax.ShapeDtypeStruct((b, 2, n, groups * dh), _BF16),
                   jax.ShapeDtypeStruct((b, n, groups * _LANES), _BF16)],
        compiler_params=_params(("parallel", "parallel", "parallel", "arbitrary"), vmem),
        name="fourier_channel_dft",
    )(h, cs_half, mid_w)


def _fourier_tables(dg):
    cc, sc = _dft_cos_sin(dg, dg ** -0.5)
    half = dg // 2
    mid_w = np.pad(cc[:, half:half + 1], ((0, 0), (0, _LANES - 1)))
    return jnp.asarray(np.stack([cc[:, :half], sc[:, :half]]), _BF16), jnp.asarray(mid_w, _BF16)


def _fourier_2d(h, tables, side_cast=None):
    b, n, d = h.shape
    cs_half, mid_w = tables
    z, mid = _channel_dft_half(h, cs_half, mid_w)
    a = _position_dft_matrix(n)[None]
    tm = _tile(n, 1024, _SUBLANES_BF16)
    e = _matmul(a, mid, tm=tm, tn=mid.shape[2], tk=_tile(n, 2048, _LANES), k_extent=n, out_dtype=_BF16,
                name="fourier_position_dft_mid")
    return _matmul(a, z.reshape(b, 2 * n, d // 2), tm=tm, tn=cs_half.shape[2], tk=_tile(n, 4096, _LANES),
                   out_dtype=_BF16, mirror=e, side_cast=side_cast, name="fourier_position_dft")


def _mla_prep_body(*refs, with_q, with_rope):
    refs = list(refs)
    a_ref = refs.pop(0)
    gq_ref = refs.pop(0) if with_q else None
    gkv_ref = refs.pop(0)
    cos_ref, sin_ref = (refs.pop(0), refs.pop(0)) if with_rope else (None, None)
    cq_ref = refs.pop(0) if with_q else None
    ckv_ref, kr_ref = refs

    def norm(t, g):
        return t * lax.rsqrt(jnp.mean(t * t, axis=-1, keepdims=True) + EPS) * g

    off = 0
    if with_q:
        cq_ref[...] = norm(a_ref[:, :Q_LORA], gq_ref[...]).astype(cq_ref.dtype)
        off = Q_LORA
    ckv_ref[...] = norm(a_ref[:, off:off + KV_LORA], gkv_ref[...]).astype(ckv_ref.dtype)
    kr = a_ref[:, off + KV_LORA:off + KV_LORA + _LANES]
    if with_rope:
        kr = _rope_rotate(kr, cos_ref[:, :_LANES], sin_ref[:, :_LANES], False)
    kr_ref[:, :_LANES] = kr.astype(kr_ref.dtype)
    kr_ref[:, _LANES:] = pltpu.roll(kr, QK_ROPE, 1).astype(kr_ref.dtype)


def _mla_prep(a, g_q, g_kv, rope):
    b, n, w = a.shape
    with_q = g_q is not None
    with_rope = rope is not None
    tr = _tile(n, 256, _SUBLANES_BF16)
    row = lambda z, i: (z, i, 0)
    in_specs = [pl.BlockSpec((None, tr, w), row)]
    args = [a]
    if with_q:
        in_specs.append(pl.BlockSpec((1, Q_LORA), lambda z, i: (0, 0)))
        args.append(g_q.reshape(1, Q_LORA))
    in_specs.append(pl.BlockSpec((1, KV_LORA), lambda z, i: (0, 0)))
    args.append(g_kv.reshape(1, KV_LORA))
    if with_rope:
        in_specs += [pl.BlockSpec((tr, _HEAD_SLOT), lambda z, i: (i, 0))] * 2
        args += list(rope)
    out_specs, out_shape = [], []
    if with_q:
        out_specs.append(pl.BlockSpec((None, tr, Q_LORA), row))
        out_shape.append(jax.ShapeDtypeStruct((b, n, Q_LORA), _BF16))
    out_specs += [pl.BlockSpec((None, tr, KV_LORA), row), pl.BlockSpec((None, tr, _HEAD_SLOT), row)]
    out_shape += [jax.ShapeDtypeStruct((b, n, KV_LORA), _BF16), jax.ShapeDtypeStruct((b, n, _HEAD_SLOT), _BF16)]
    vmem = 6 * _nbytes((tr, w), _F32) + (2 << 20)
    return pl.pallas_call(
        functools.partial(_mla_prep_body, with_q=with_q, with_rope=with_rope),
        grid=(b, n // tr),
        in_specs=in_specs,
        out_specs=out_specs,
        out_shape=out_shape,
        compiler_params=_params(("parallel", "parallel"), vmem),
        name="mla_prep_latent" if with_q else "mla_prep_context",
    )(*args)


def _attn_body(q_ref, k_ref, kr_ref, vt_ref, o_ref, kcat_ref, vext_ref, *, chunks):
    @pl.when(pl.program_id(2) == 0)
    def _():
        kcat_ref[:, :QK_NOPE] = k_ref[...]
        kcat_ref[:, QK_NOPE:] = kr_ref[...]
        vext_ref[:V_HEAD, :] = vt_ref[...]
        vext_ref[V_HEAD:, :] = jnp.ones((_SUBLANES_BF16, vext_ref.shape[1]), vext_ref.dtype)

    starts = [sum(chunks[:c]) for c in range(len(chunks))]
    slab = min(_ATTN_QUERY_SLAB, q_ref.shape[0])
    n_slabs = q_ref.shape[0] // slab
    items = [(c, j) for c in range(len(chunks)) for j in range(n_slabs)]

    def scores(c, j):
        return lax.dot_general(kcat_ref[starts[c]:starts[c] + chunks[c], :], q_ref[j * slab:(j + 1) * slab, :],
                               (((1,), (1,)), ((), ())), preferred_element_type=_F32)

    m, acc = [None] * n_slabs, [None] * n_slabs
    ahead = [scores(*items[i]) for i in range(min(_ATTN_LOOKAHEAD, len(items)))]
    for idx, (c, j) in enumerate(items):
        st = ahead.pop(0)
        if idx + _ATTN_LOOKAHEAD < len(items):
            ahead.append(scores(*items[idx + _ATTN_LOOKAHEAD]))
        mc = jnp.max(st, axis=0, keepdims=True).astype(_BF16)
        m_new = mc if c == 0 else jnp.maximum(m[j], mc)
        p = jnp.exp2(st.astype(_BF16) - m_new)
        pv = jnp.dot(vext_ref[:, starts[c]:starts[c] + chunks[c]], p, preferred_element_type=_F32)
        acc[j] = pv if c == 0 else jnp.exp2(m[j].astype(_F32) - m_new.astype(_F32)) * acc[j] + pv
        m[j] = m_new
    for j in range(n_slabs):
        o_ref[j * slab:(j + 1) * slab, :] = (acc[j][:V_HEAD] / acc[j][V_HEAD:V_HEAD + 1]).T.astype(o_ref.dtype)


def _attention(q, k, kr, vt):
    b, n, _ = q.shape
    nk = k.shape[1]
    tq = _tile(n, 4096, _LANES)
    chunk = _ATTN_KEY_CHUNK
    chunks = [chunk] * (nk // chunk) + ([nk % chunk] if nk % chunk else [])
    vmem = (4 * _nbytes((tq, _HEAD_SLOT), _BF16) + 5 * _nbytes((nk, _LANES), _BF16)
            + 2 * _nbytes((V_HEAD, nk), _BF16) + 6 * _nbytes((min(chunk, nk), tq), _F32) + (4 << 20))
    return pl.pallas_call(
        functools.partial(_attn_body, chunks=chunks),
        grid=(b, MLA_HEADS, n // tq),
        in_specs=[pl.BlockSpec((None, tq, _HEAD_SLOT), lambda z, h, i: (z, i, h)),
                  pl.BlockSpec((None, nk, QK_NOPE), lambda z, h, i: (z, 0, h)),
                  pl.BlockSpec((None, nk, _LANES), lambda z, h, i: (z, 0, h % 2)),
                  pl.BlockSpec((None, V_HEAD, nk), lambda z, h, i: (z, h, 0))],
        out_specs=pl.BlockSpec((None, tq, V_HEAD), lambda z, h, i: (z, i, h)),
        out_shape=jax.ShapeDtypeStruct((b, n, MLA_HEADS * V_HEAD), _BF16),
        scratch_shapes=[pltpu.VMEM((nk, _HEAD_SLOT), _BF16), pltpu.VMEM((V_HEAD + _SUBLANES_BF16, nk), _BF16)],
        compiler_params=_params(("parallel", "parallel", "arbitrary"), vmem),
        name="mla_attention",
    )(q, k, kr, vt)


def _rope_tables(n):
    rows_n = n // GRID_W
    row = jnp.repeat(jnp.arange(rows_n, dtype=_F32), GRID_W)
    col = jnp.tile(jnp.arange(GRID_W, dtype=_F32), rows_n)
    n_freq = QK_ROPE // 4
    inv_freq = ROPE_BASE ** (-jnp.arange(n_freq, dtype=_F32) / n_freq)
    ar, ac = row[:, None] * inv_freq, col[:, None] * inv_freq
    zeros = jnp.zeros((n, _LANES - QK_ROPE), _F32)
    cos = jnp.concatenate([jnp.cos(ar), jnp.cos(ar), jnp.cos(ac), jnp.cos(ac)], axis=1)
    sin = jnp.concatenate([-jnp.sin(ar), jnp.sin(ar), -jnp.sin(ac), jnp.sin(ac)], axis=1)
    return (jnp.concatenate([cos, zeros, zeros, cos], axis=1), jnp.concatenate([sin, zeros, zeros, sin], axis=1))


def _mla_mix(h, hc, w_a, g_q, g_kv, w_uq, w_ukv, w_o, w_o_sel, x, gate):
    b, n, d = h.shape
    nc = hc.shape[1]
    rope = _rope_tables(n)
    a_width = Q_LORA + KV_LORA + QK_ROPE
    a_pad = -(-a_width // _HEAD_SLOT) * _HEAD_SLOT
    w_a_pad = jnp.pad(w_a, ((0, 0), (0, a_pad - a_width))).astype(_BF16)[None]
    tm = _tile(n, 1024, _SUBLANES_BF16)
    a_l = _matmul(h, w_a_pad, tm=tm, tn=_tile(a_pad, 3 * _HEAD_SLOT, _HEAD_SLOT), out_dtype=_F32,
                  name="mla_down_latent")
    a_c = _matmul(hc, w_a_pad[:, :, Q_LORA:], tm=_tile(nc, 1024, _SUBLANES_BF16), tn=_HEAD_SLOT,
                  out_dtype=_F32, name="mla_down_context")
    cq, ckv_l, kr_l = _mla_prep(a_l, g_q, g_kv, rope)
    ckv_c, kr_c = _mla_prep(a_c, None, g_kv, None)
    ckv = jnp.concatenate([ckv_c, ckv_l], axis=1)
    kr = jnp.concatenate([kr_c, kr_l], axis=1)

    head = QK_NOPE + QK_ROPE
    q, w_o_bf16 = _matmul(cq, w_uq[None], tm=tm, tn=_tile(MLA_HEADS * head, 4 * head, 2 * head), out_dtype=_BF16,
                          rope=rope, q_scale=float(head ** -0.5 * _LOG2E), side_cast=(w_o, w_o_sel),
                          name="mla_q_up")
    nkv = nc + n
    w_ukv_h = w_ukv.reshape(KV_LORA, MLA_HEADS // 2, 2, QK_NOPE + V_HEAD)
    w_uk_h = w_ukv_h[..., :QK_NOPE]
    w_uk_h = jnp.stack([w_uk_h[:, :, 0], jnp.roll(w_uk_h[:, :, 1], -(QK_NOPE // 2), axis=-1)], axis=2)
    w_uk = w_uk_h.reshape(KV_LORA, MLA_HEADS * QK_NOPE).astype(_BF16)[None]
    w_ukv_h = w_ukv_h.reshape(KV_LORA, MLA_HEADS, QK_NOPE + V_HEAD)
    w_uv_t = w_ukv_h[:, :, QK_NOPE:].reshape(KV_LORA, MLA_HEADS * V_HEAD).T.astype(_BF16)[None]
    k = _matmul(ckv, w_uk, tm=_tile(nkv, 1088, _SUBLANES_BF16), tn=_tile(MLA_HEADS * QK_NOPE, 2048, _LANES),
                out_dtype=_BF16, name="mla_k_up")
    vt = _matmul(w_uv_t, ckv, nt=True, tm=_tile(MLA_HEADS * V_HEAD, 1024, _SUBLANES_BF16),
                 tn=_tile(nkv, 2176, _LANES), out_dtype=_BF16, name="mla_v_up_transposed")
    o = _attention(q, k, kr, vt)
    return _matmul(o, w_o_bf16[None], tm=_tile(n, 512, _SUBLANES_BF16), tn=_tile(d, 512, _LANES),
                   out_dtype=_F32, resid=x, gate=gate, name="mla_out_proj")


def _ffn(x, h, w_gate, w_up, w_down, layer, gate, w_down_bf16=None):
    b, n, d = x.shape
    hidden = w_gate.shape[2]
    tm = _tile(n, 1024, _SUBLANES_BF16)
    u = _matmul(h, w_gate, b2=w_up, b_sel=layer, tm=tm, tn=_tile(hidden, 256, _LANES), out_dtype=_BF16,
                side_cast=None if w_down_bf16 is not None else (w_down, layer), name="ffn_gate_up")
    if w_down_bf16 is None:
        u, w_down_bf16 = u
    out = _matmul(u, w_down_bf16[None], tm=_tile(n, 512, _SUBLANES_BF16), tn=_tile(d, 512, _LANES),
                  out_dtype=_F32, resid=x, gate=gate, name="ffn_down")
    return out, w_down_bf16


def kernel(x, c, ctx, c_ctx, w_ada, b_ada, g_mix, g_ffn, fourier_w_out, mla_w_a, mla_g_q, mla_g_kv, mla_w_uq,
           mla_w_ukv, mla_w_o, w_gate, w_up, w_down, g_final):
    b, n, d = x.shape
    depth = w_ada.shape[0]
    cond = jnp.concatenate([c, c_ctx[None]], axis=0)
    cond = jnp.pad(cond, ((0, -(b + 1) % 8), (0, 0)))
    mod = _ada_modulation(cond, w_ada, b_ada)
    dg = d // FOURIER_GROUPS
    tables = _fourier_tables(dg)

    xc = ctx
    for i in range(depth):
        need_ctx = i < depth - 1
        j = i // 2
        lat = [mod[i, :b, s * d:(s + 1) * d].reshape(b, 1, d) for s in range(N_MOD)]
        con = [mod[i, b:b + 1, s * d:(s + 1) * d].reshape(1, 1, d) for s in range(N_MOD)]
        h = _rms_norm(x, g_mix[i], lat[1], lat[0], out_dtype=_BF16)
        hc = _rms_norm(xc, g_mix[i], con[1], con[0], out_dtype=_BF16)
        tm = _tile(n, 1024, _SUBLANES_BF16)
        if i % 2 == 0:
            f, w_out = _fourier_2d(h, tables, side_cast=(fourier_w_out, j))
            w_out = w_out[None]
            x = _matmul(f, w_out, tm=tm, tn=_tile(d, 512, _LANES), out_dtype=_F32, resid=x, gate=lat[2],
                        name="fourier_out_proj")
            if need_ctx:
                xc = _matmul(_fourier_2d(hc, tables), w_out, tm=_tile(xc.shape[1], 1024, _SUBLANES_BF16),
                             tn=_tile(d, 512, _LANES), out_dtype=_F32, resid=xc, gate=con[2],
                             name="fourier_out_proj_ctx")
        else:
            assert not need_ctx, "context queries are only implemented for the Fourier mixer layers"
            x = _mla_mix(h, hc, mla_w_a[j], mla_g_q[j], mla_g_kv[j], mla_w_uq[j], mla_w_ukv[j], mla_w_o, j,
                         x, lat[2])
        x, w_down_i = _ffn(x, _rms_norm(x, g_ffn[i], lat[4], lat[3], out_dtype=_BF16), w_gate, w_up, w_down, i,
                           lat[5])
        if need_ctx:
            hc2 = _rms_norm(xc, g_ffn[i], con[4], con[3], out_dtype=_BF16)
            nc = xc.shape[1]
            xc, _ = _ffn(xc.reshape(1, b * nc, d), hc2.reshape(1, b * nc, d), w_gate, w_up, w_down, i, con[5],
                         w_down_bf16=w_down_i)
            xc = xc.reshape(b, nc, d)
    return _rms_norm(x, g_final, out_dtype=x.dtype)
```

```python
import functools
import math

import jax
import jax.numpy as jnp
import numpy as np
from jax import lax
from jax.experimental import pallas as pl
from jax.experimental.pallas import tpu as pltpu

_F32 = jnp.float32
_BF16 = jnp.bfloat16

GRID_W = 64
FOURIER_GROUPS = 4
MLA_HEADS = 64
Q_LORA = 1536
KV_LORA = 512
QK_NOPE = 128
QK_ROPE = 64
V_HEAD = 128
ROPE_BASE = 10000.0
EPS = 1e-6
N_MOD = 6

_LANES = 128
_SUBLANES_BF16 = 16
_VMEM_CAP_BYTES = 56 * 1024 * 1024
_HEAD_SLOT = 2 * _LANES
_MM_ROW_SLAB = 256
_ATTN_KEY_CHUNK = 256
_ATTN_QUERY_SLAB = 256
_ATTN_LOOKAHEAD = 6
_LOG2E = math.log2(math.e)


def _tile(dim, pref, align):
    if dim <= pref:
        return dim
    t = (pref // align) * align
    while t >= align:
        if dim % t == 0:
            return t
        t -= align
    return dim


def _nbytes(shape, dtype):
    return int(np.prod(shape)) * jnp.dtype(dtype).itemsize


def _params(semantics, vmem_bytes):
    return pltpu.CompilerParams(dimension_semantics=semantics,
                                vmem_limit_bytes=int(min(_VMEM_CAP_BYTES, vmem_bytes)))


def _ada_body(c_ref, w_ref, b_ref, o_ref):
    c = c_ref[...]
    s = c * jax.nn.sigmoid(c)
    acc = jnp.dot(s.astype(_BF16), w_ref[...].astype(_BF16), preferred_element_type=_F32)
    o_ref[...] = acc + b_ref[...]


def _ada_modulation(cc, w_ada, b_ada):
    depth, d, n = w_ada.shape
    r = cc.shape[0]
    tn = _tile(n, 512, _LANES)
    vmem = 2 * _nbytes((d, tn), _F32) + _nbytes((d, tn), _BF16) + 8 * _nbytes((r, tn), _F32) + (4 << 20)
    return pl.pallas_call(
        _ada_body,
        grid=(depth, n // tn),
        in_specs=[
            pl.BlockSpec((r, d), lambda l, j: (0, 0)),
            pl.BlockSpec((None, d, tn), lambda l, j: (l, 0, j)),
            pl.BlockSpec((None, 1, tn), lambda l, j: (l, 0, j)),
        ],
        out_specs=pl.BlockSpec((None, r, tn), lambda l, j: (l, 0, j)),
        out_shape=jax.ShapeDtypeStruct((depth, r, n), _F32),
        compiler_params=_params(("parallel", "parallel"), vmem),
        name="ada_modulation",
    )(cc, w_ada, b_ada.reshape(depth, 1, n))


def _norm_body(*refs, modulate):
    if modulate:
        x_ref, g_ref, sc_ref, sh_ref, o_ref = refs
    else:
        x_ref, g_ref, o_ref = refs
    x = x_ref[...]
    y = x * lax.rsqrt(jnp.mean(x * x, axis=-1, keepdims=True) + EPS) * g_ref[...]
    if modulate:
        y = y * (1.0 + sc_ref[...]) + sh_ref[...]
    o_ref[...] = y.astype(o_ref.dtype)


def _rms_norm(x, g, scale=None, shift=None, *, out_dtype):
    b, n, d = x.shape
    tr = _tile(n, 256, 8)
    modulate = scale is not None
    in_specs = [pl.BlockSpec((None, tr, d), lambda bb, i: (bb, i, 0)),
                pl.BlockSpec((1, d), lambda bb, i: (0, 0))]
    args = [x, g.reshape(1, d)]
    if modulate:
        per_batch = scale.shape[0] > 1
        mod_spec = pl.BlockSpec((None, 1, d), (lambda bb, i: (bb, 0, 0)) if per_batch else (lambda bb, i: (0, 0, 0)))
        in_specs += [mod_spec, mod_spec]
        args += [scale, shift]
    vmem = 2 * _nbytes((tr, d), _F32) + 2 * _nbytes((tr, d), out_dtype) + 3 * _nbytes((tr, d), _F32) + (2 << 20)
    return pl.pallas_call(
        functools.partial(_norm_body, modulate=modulate),
        grid=(b, n // tr),
        in_specs=in_specs,
        out_specs=pl.BlockSpec((None, tr, d), lambda bb, i: (bb, i, 0)),
        out_shape=jax.ShapeDtypeStruct((b, n, d), out_dtype),
        compiler_params=_params(("parallel", "parallel"), vmem),
        name="rms_norm_modulate" if modulate else "rms_norm",
    )(*args)


def _rope_rotate(t, cos, sin, upper_half):
    lane = lax.broadcasted_iota(jnp.int32, t.shape, 1)
    quarter = QK_ROPE // 4
    even_chunk = (lane // quarter) % 2 == 0
    swapped = jnp.where(even_chunk, pltpu.roll(t, _LANES - quarter, 1), pltpu.roll(t, quarter, 1))
    valid = (lane >= QK_ROPE) if upper_half else (lane < QK_ROPE)
    return jnp.where(valid, t * cos + swapped * sin, 0.0)


def _q_head_slots(acc, cos, sin, q_scale):
    tiles = [acc[:, i * _LANES:(i + 1) * _LANES] for i in range(acc.shape[1] // _LANES)]
    low_half = lax.broadcasted_iota(jnp.int32, tiles[0].shape, 1) < QK_ROPE
    cos_lo, cos_hi = cos[:, :_LANES], cos[:, _LANES:]
    sin_lo, sin_hi = sin[:, :_LANES], sin[:, _LANES:]
    parts = []
    for p in range(len(tiles) // 3):
        t0, t1, t2 = tiles[3 * p:3 * p + 3]
        parts += [t0 * q_scale, _rope_rotate(t1, cos_lo, sin_lo, False) * q_scale,
                  jnp.where(low_half, t2, t1) * q_scale, _rope_rotate(t2, cos_hi, sin_hi, True) * q_scale]
    return jnp.concatenate(parts, axis=1)


def _side_cast_blocks(rows, cols, steps):
    widths = [c for c in range(_LANES, cols + 1, _LANES) if cols % c == 0]
    for bc in sorted(widths, key=lambda c: abs(c - 4 * _LANES)):
        ncb = cols // bc
        if steps % ncb == 0 and rows % (steps // ncb) == 0 and (rows // (steps // ncb)) % _SUBLANES_BF16 == 0:
            return rows // (steps // ncb), bc
    raise ValueError(f"no tile-aligned split of {(rows, cols)} into {steps} blocks")


def _mm_body(*refs, n_b, epilogue, nk, q_scale, nt, side):
    a_ref = refs[0]
    b_refs = refs[1:1 + n_b]
    pos = 1 + n_b
    extra = ()
    if epilogue in ("resgate", "qrope"):
        extra = refs[pos:pos + 2]
        pos += 2
    elif epilogue == "mirror":
        extra = refs[pos:pos + 2]
        pos += 2
    if side:
        refs[pos + 2][...] = refs[pos][...].astype(refs[pos + 2].dtype)
        refs = refs[:pos] + refs[pos + 1:pos + 2] + refs[pos + 3:]
    o_ref = refs[pos]
    acc_refs = refs[pos + 1:]

    def finish(accs, rows=slice(None)):
        if epilogue == "plain":
            out = accs[0]
        elif epilogue == "swiglu":
            g, u = accs
            out = g * jax.nn.sigmoid(g) * u
        elif epilogue == "resgate":
            res_ref, gate_ref = extra
            out = res_ref[...] + gate_ref[...] * accs[0]
        elif epilogue == "qrope":
            cos_ref, sin_ref = extra
            out = _q_head_slots(accs[0], cos_ref[rows, :], sin_ref[rows, :], q_scale)
        elif epilogue == "mirror":
            p, neg_q = accs
            e_ref, perm_ref = extra
            plus = p - neg_q
            first = plus[:, :_LANES]
            lane = lax.broadcasted_iota(jnp.int32, first.shape, 1)
            first = jnp.where(lane == 0, e_ref[...].astype(_F32), first)
            plus = jnp.concatenate([first, plus[:, _LANES:]], axis=1).astype(_BF16)
            upper = jnp.dot(plus, perm_ref[...], preferred_element_type=_F32)
            out = jnp.concatenate([p + neg_q, upper], axis=1)
        o_ref[rows, :] = out.astype(o_ref.dtype)

    def dots(rows=slice(None)):
        a = a_ref[rows, :]
        outs = []
        for b_ref in b_refs:
            b = b_ref[...].astype(_BF16)
            if nt:
                outs.append(lax.dot_general(a, b, (((1,), (1,)), ((), ())), preferred_element_type=_F32))
            else:
                outs.append(jnp.dot(a, b, preferred_element_type=_F32))
        return outs

    if nk == 1 and epilogue == "qrope" and a_ref.shape[0] % _MM_ROW_SLAB == 0:
        for r in range(a_ref.shape[0] // _MM_ROW_SLAB):
            rows = slice(r * _MM_ROW_SLAB, (r + 1) * _MM_ROW_SLAB)
            finish(dots(rows), rows)
        return
    if nk == 1:
        finish(dots())
        return

    k = pl.program_id(3)

    if epilogue == "mirror":
        half = nk // 2
        for idx, first_k in ((0, 0), (1, half)):
            @pl.when(k == first_k)
            def _(idx=idx):
                acc_refs[idx][...] = dots()[0]

            @pl.when((k > first_k) & (k < first_k + half))
            def _(idx=idx):
                acc_refs[idx][...] += dots()[0]
    else:
        @pl.when(k == 0)
        def _():
            for acc_ref, d in zip(acc_refs, dots()):
                acc_ref[...] = d

        @pl.when(k > 0)
        def _():
            for acc_ref, d in zip(acc_refs, dots()):
                acc_ref[...] += d

    @pl.when(k == nk - 1)
    def _():
        finish([acc_ref[...] for acc_ref in acc_refs])


def _matmul(a, b, *, tm, tn, tk=None, out_dtype, b2=None, b_sel=None, nt=False, resid=None, gate=None,
            rope=None, q_scale=1.0, side_cast=None, mirror=None, k_extent=None, name):
    ba, m, kdim = a.shape
    kdim = kdim if k_extent is None else k_extent
    lb = b.shape[0]
    n = b.shape[1] if nt else b.shape[2]
    b_batched = b_sel is None and lb > 1
    bt = max(ba, lb if b_batched else 1, resid.shape[0] if resid is not None else 1)
    tk = kdim if tk is None else tk
    nk = kdim // tk
    assert m % tm == 0 and n % tn == 0 and kdim % tk == 0, (name, a.shape, b.shape, tm, tn, tk)

    def bidx(count):
        return (lambda z: z) if count > 1 else (lambda z: 0)

    ai = bidx(ba)
    if b_sel is not None:
        bi = lambda z: b_sel
    else:
        bi = bidx(lb)
    if nt:
        b_spec = pl.BlockSpec((None, tn, tk), lambda z, i, j, k: (bi(z), j, k))
        b_block = (tn, tk)
    else:
        b_spec = pl.BlockSpec((None, tk, tn), lambda z, i, j, k: (bi(z), k, j))
        b_block = (tk, tn)
    in_specs = [pl.BlockSpec((None, tm, tk), lambda z, i, j, k: (ai(z), i, k)), b_spec]
    args = [a, b]
    n_b = 1
    epilogue = "plain"
    blocks = [((tm, tk), a.dtype), (b_block, b.dtype)]
    if b2 is not None:
        in_specs.append(b_spec)
        args.append(b2)
        blocks.append((b_block, b2.dtype))
        n_b = 2
        epilogue = "swiglu"
    if resid is not None:
        gi = bidx(gate.shape[0])
        in_specs += [pl.BlockSpec((None, tm, tn), lambda z, i, j, k: (z, i, j)),
                     pl.BlockSpec((None, 1, tn), lambda z, i, j, k: (gi(z), 0, j))]
        args += [resid, gate]
        blocks.append(((tm, tn), resid.dtype))
        epilogue = "resgate"
    out_tn, out_n = tn, n
    if rope is not None:
        head = QK_NOPE + QK_ROPE
        assert tn % (2 * head) == 0 and nk == 1
        out_tn, out_n = tn // head * _HEAD_SLOT, n // head * _HEAD_SLOT
        table_spec = pl.BlockSpec((tm, _HEAD_SLOT), lambda z, i, j, k: (i, 0))
        in_specs += [table_spec, table_spec]
        args += list(rope)
        blocks += [((tm, _HEAD_SLOT), _F32)] * 2
        epilogue = "qrope"
    n_acc = n_b
    if mirror is not None:
        assert nk % 2 == 0 and tn % _LANES == 0
        out_tn, out_n, n_acc = 2 * tn, 2 * n, 2
        unmirror = np.zeros((tn, tn), np.float32)
        unmirror[0, 0] = 1.0
        unmirror[np.arange(1, tn), tn - np.arange(1, tn)] = 1.0
        in_specs += [pl.BlockSpec((None, tm, _LANES), lambda z, i, j, k: (z, i, j)),
                     pl.BlockSpec((tn, tn), lambda z, i, j, k: (0, 0))]
        args += [mirror, jnp.asarray(unmirror, _BF16)]
        blocks += [((tm, _LANES), mirror.dtype), ((tn, tn), _BF16)]
        epilogue = "mirror"
    blocks.append(((tm, out_tn), out_dtype))
    out_specs = pl.BlockSpec((None, tm, out_tn), lambda z, i, j, k: (z, i, j))
    out_shape = jax.ShapeDtypeStruct((bt, m, out_n), out_dtype)
    if side_cast is not None:
        src, sel = side_cast
        steps_i, steps_j = m // tm, n // tn
        br, bc = _side_cast_blocks(src.shape[1], src.shape[2], bt * steps_i * steps_j)
        ncb = src.shape[2] // bc

        def step(z, i, j):
            return (z * steps_i + i) * steps_j + j

        in_specs.append(pl.BlockSpec((None, br, bc), lambda z, i, j, k: (sel, step(z, i, j) // ncb, step(z, i, j) % ncb)))
        args.append(src)
        out_specs = [out_specs, pl.BlockSpec((br, bc), lambda z, i, j, k: (step(z, i, j) // ncb, step(z, i, j) % ncb))]
        out_shape = [out_shape, jax.ShapeDtypeStruct(src.shape[1:], _BF16)]
        blocks += [((br, bc), src.dtype), ((br, bc), _BF16)]
    scratch = [pltpu.VMEM((tm, tn), _F32) for _ in range(n_acc)] if nk > 1 else []
    cast_tmp = n_b * _nbytes(b_block, _BF16) if b.dtype != _BF16 else 0
    vmem = (2 * sum(_nbytes(s, dt) for s, dt in blocks) + cast_tmp + (n_b + 2) * _nbytes((tm, tn), _F32)
            + (2 << 20))
    return pl.pallas_call(
        functools.partial(_mm_body, n_b=n_b, epilogue=epilogue, nk=nk, q_scale=q_scale, nt=nt,
                          side=side_cast is not None),
        grid=(bt, m // tm, n // tn, nk),
        in_specs=in_specs,
        out_specs=out_specs,
        out_shape=out_shape,
        scratch_shapes=scratch,
        compiler_params=_params(("parallel", "parallel", "parallel", "arbitrary"), vmem),
        name=name,
    )(*args)


def _dft_cos_sin(n, scale):
    idx = np.arange(n)
    ang = ((idx[:, None] * idx[None, :]) % n) * (2.0 * np.pi / n)
    return np.cos(ang) * scale, np.sin(ang) * scale


def _dft_gen_body(ca_ref, sa_ref, cb_ref, sb_ref, o_ref, *, n, scale):
    ca, sa, cb, sb = ca_ref[...], sa_ref[...], cb_ref[...], sb_ref[...]
    o_ref[:, :n] = ((ca * cb - sa * sb) * scale).astype(o_ref.dtype)
    o_ref[:, n:] = ((sa * cb + ca * sb) * -scale).astype(o_ref.dtype)


def _position_dft_matrix(n):
    tr = _tile(n, 256, _SUBLANES_BF16)
    t = jnp.arange(n, dtype=jnp.int32)[None, :]
    two_pi_n = 2.0 * np.pi / n
    beta = ((jnp.arange(tr, dtype=jnp.int32)[:, None] * t) % n).astype(_F32) * two_pi_n
    alpha = ((jnp.arange(0, n, tr, dtype=jnp.int32)[:, None] * t) % n).astype(_F32) * two_pi_n
    alpha = alpha.reshape(n // tr, 1, n)
    row_spec = pl.BlockSpec((None, 1, n), lambda i: (i, 0, 0))
    base_spec = pl.BlockSpec((tr, n), lambda i: (0, 0))
    vmem = 4 * _nbytes((tr, n), _F32) + 2 * _nbytes((tr, 2 * n), _BF16) + 4 * _nbytes((tr, n), _F32) + (2 << 20)
    return pl.pallas_call(
        functools.partial(_dft_gen_body, n=n, scale=float(n ** -0.5)),
        grid=(n // tr,),
        in_specs=[row_spec, row_spec, base_spec, base_spec],
        out_specs=pl.BlockSpec((tr, 2 * n), lambda i: (i, 0)),
        out_shape=jax.ShapeDtypeStruct((n, 2 * n), _BF16),
        compiler_params=_params(("parallel",), vmem),
        name="fourier_position_matrix",
    )(jnp.cos(alpha), jnp.sin(alpha), jnp.cos(beta), jnp.sin(beta))


def _channel_dft_body(h_ref, cs_ref, mid_w_ref, z_ref, mid_ref):
    h = h_ref[...]
    z_ref[...] = jnp.dot(h, cs_ref[...], preferred_element_type=_F32).astype(z_ref.dtype)

    @pl.when(pl.program_id(3) == 0)
    def _():
        mid_ref[...] = jnp.dot(h, mid_w_ref[...], preferred_element_type=_F32).astype(mid_ref.dtype)


def _channel_dft_half(h, cs_half, mid_w):
    b, n, d = h.shape
    dg, dh = cs_half.shape[1:]
    groups = d // dg
    tm = _tile(n, 1024, _SUBLANES_BF16)
    vmem = (2 * (_nbytes((tm, dg), _BF16) + _nbytes((dg, dh), _BF16) + _nbytes((dg, _LANES), _BF16)
                 + _nbytes((tm, dh), _BF16) + _nbytes((tm, _LANES), _BF16)) + 3 * _nbytes((tm, dh), _F32) + (2 << 20))
    return pl.pallas_call(
        _channel_dft_body,
        grid=(b, n // tm, groups, 2),
        in_specs=[pl.BlockSpec((None, tm, dg), lambda z, i, g, c: (z, i, g)),
                  pl.BlockSpec((None, dg, dh), lambda z, i, g, c: (c, 0, 0)),
                  pl.BlockSpec((dg, _LANES), lambda z, i, g, c: (0, 0))],
        out_specs=[pl.BlockSpec((None, None, tm, dh), lambda z, i, g, c: (z, c, i, g)),
                   pl.BlockSpec((None, tm, _LANES), lambda z, i, g, c: (z, i, g))],
        out_shape=[jax.ShapeDtypeStruct((b, 2, n, groups * dh), _BF16),
                   jax.ShapeDtypeStruct((b, n, groups * _LANES), _BF16)],
        compiler_params=_params(("parallel", "parallel", "parallel", "arbitrary"), vmem),
        name="fourier_channel_dft",
    )(h, cs_half, mid_w)


def _fourier_tables(dg):
    cc, sc = _dft_cos_sin(dg, dg ** -0.5)
    half = dg // 2
    mid_w = np.pad(cc[:, half:half + 1], ((0, 0), (0, _LANES - 1)))
    return jnp.asarray(np.stack([cc[:, :half], sc[:, :half]]), _BF16), jnp.asarray(mid_w, _BF16)


def _fourier_2d(h, tables, side_cast=None):
    b, n, d = h.shape
    cs_half, mid_w = tables
    z, mid = _channel_dft_half(h, cs_half, mid_w)
    a = _position_dft_matrix(n)[None]
    tm = _tile(n, 1024, _SUBLANES_BF16)
    e = _matmul(a, mid, tm=tm, tn=mid.shape[2], tk=_tile(n, 2048, _LANES), k_extent=n, out_dtype=_BF16,
                name="fourier_position_dft_mid")
    return _matmul(a, z.reshape(b, 2 * n, d // 2), tm=tm, tn=cs_half.shape[2], tk=_tile(n, 4096, _LANES),
                   out_dtype=_BF16, mirror=e, side_cast=side_cast, name="fourier_position_dft")


def _mla_prep_body(*refs, with_q, with_rope):
    refs = list(refs)
    a_ref = refs.pop(0)
    gq_ref = refs.pop(0) if with_q else None
    gkv_ref = refs.pop(0)
    cos_ref, sin_ref = (refs.pop(0), refs.pop(0)) if with_rope else (None, None)
    cq_ref = refs.pop(0) if with_q else None
    ckv_ref, kr_ref = refs

    def norm(t, g):
        return t * lax.rsqrt(jnp.mean(t * t, axis=-1, keepdims=True) + EPS) * g

    off = 0
    if with_q:
        cq_ref[...] = norm(a_ref[:, :Q_LORA], gq_ref[...]).astype(cq_ref.dtype)
        off = Q_LORA
    ckv_ref[...] = norm(a_ref[:, off:off + KV_LORA], gkv_ref[...]).astype(ckv_ref.dtype)
    kr = a_ref[:, off + KV_LORA:off + KV_LORA + _LANES]
    if with_rope:
        kr = _rope_rotate(kr, cos_ref[:, :_LANES], sin_ref[:, :_LANES], False)
    kr_ref[:, :_LANES] = kr.astype(kr_ref.dtype)
    kr_ref[:, _LANES:] = pltpu.roll(kr, QK_ROPE, 1).astype(kr_ref.dtype)


def _mla_prep(a, g_q, g_kv, rope):
    b, n, w = a.shape
    with_q = g_q is not None
    with_rope = rope is not None
    tr = _tile(n, 256, _SUBLANES_BF16)
    row = lambda z, i: (z, i, 0)
    in_specs = [pl.BlockSpec((None, tr, w), row)]
    args = [a]
    if with_q:
        in_specs.append(pl.BlockSpec((1, Q_LORA), lambda z, i: (0, 0)))
        args.append(g_q.reshape(1, Q_LORA))
    in_specs.append(pl.BlockSpec((1, KV_LORA), lambda z, i: (0, 0)))
    args.append(g_kv.reshape(1, KV_LORA))
    if with_rope:
        in_specs += [pl.BlockSpec((tr, _HEAD_SLOT), lambda z, i: (i, 0))] * 2
        args += list(rope)
    out_specs, out_shape = [], []
    if with_q:
        out_specs.append(pl.BlockSpec((None, tr, Q_LORA), row))
        out_shape.append(jax.ShapeDtypeStruct((b, n, Q_LORA), _BF16))
    out_specs += [pl.BlockSpec((None, tr, KV_LORA), row), pl.BlockSpec((None, tr, _HEAD_SLOT), row)]
    out_shape += [jax.ShapeDtypeStruct((b, n, KV_LORA), _BF16), jax.ShapeDtypeStruct((b, n, _HEAD_SLOT), _BF16)]
    vmem = 6 * _nbytes((tr, w), _F32) + (2 << 20)
    return pl.pallas_call(
        functools.partial(_mla_prep_body, with_q=with_q, with_rope=with_rope),
        grid=(b, n // tr),
        in_specs=in_specs,
        out_specs=out_specs,
        out_shape=out_shape,
        compiler_params=_params(("parallel", "parallel"), vmem),
        name="mla_prep_latent" if with_q else "mla_prep_context",
    )(*args)


def _attn_body(q_ref, k_ref, kr_ref, vt_ref, o_ref, kcat_ref, vext_ref, *, chunks):
    @pl.when(pl.program_id(2) == 0)
    def _():
        kcat_ref[:, :QK_NOPE] = k_ref[...]
        kcat_ref[:, QK_NOPE:] = kr_ref[...]
        vext_ref[:V_HEAD, :] = vt_ref[...]
        vext_ref[V_HEAD:, :] = jnp.ones((_SUBLANES_BF16, vext_ref.shape[1]), vext_ref.dtype)

    starts = [sum(chunks[:c]) for c in range(len(chunks))]
    slab = min(_ATTN_QUERY_SLAB, q_ref.shape[0])
    n_slabs = q_ref.shape[0] // slab
    items = [(c, j) for c in range(len(chunks)) for j in range(n_slabs)]

    def scores(c, j):
        return lax.dot_general(kcat_ref[starts[c]:starts[c] + chunks[c], :], q_ref[j * slab:(j + 1) * slab, :],
                               (((1,), (1,)), ((), ())), preferred_element_type=_F32)

    m, acc = [None] * n_slabs, [None] * n_slabs
    ahead = [scores(*items[i]) for i in range(min(_ATTN_LOOKAHEAD, len(items)))]
    for idx, (c, j) in enumerate(items):
        st = ahead.pop(0)
        if idx + _ATTN_LOOKAHEAD < len(items):
            ahead.append(scores(*items[idx + _ATTN_LOOKAHEAD]))
        mc = jnp.max(st, axis=0, keepdims=True).astype(_BF16)
        m_new = mc if c == 0 else jnp.maximum(m[j], mc)
        p = jnp.exp2(st.astype(_BF16) - m_new)
        pv = jnp.dot(vext_ref[:, starts[c]:starts[c] + chunks[c]], p, preferred_element_type=_F32)
        acc[j] = pv if c == 0 else jnp.exp2(m[j].astype(_F32) - m_new.astype(_F32)) * acc[j] + pv
        m[j] = m_new
    for j in range(n_slabs):
        o_ref[j * slab:(j + 1) * slab, :] = (acc[j][:V_HEAD] / acc[j][V_HEAD:V_HEAD + 1]).T.astype(o_ref.dtype)


def _attention(q, k, kr, vt):
    b, n, _ = q.shape
    nk = k.shape[1]
    tq = _tile(n, 4096, _LANES)
    chunk = _ATTN_KEY_CHUNK
    chunks = [chunk] * (nk // chunk) + ([nk % chunk] if nk % chunk else [])
    slab = min(_ATTN_QUERY_SLAB, tq)
    vmem = (2 * _nbytes((tq, _HEAD_SLOT), _BF16) + 2 * _nbytes((tq, V_HEAD), _BF16)
            + 8 * _nbytes((nk, _LANES), _BF16) + 3 * _nbytes((V_HEAD + _SUBLANES_BF16, nk), _BF16)
            + (2 * _ATTN_LOOKAHEAD + 4) * _nbytes((min(chunk, nk), slab), _F32)
            + (tq // slab) * _nbytes((V_HEAD + _SUBLANES_BF16, slab), _F32) + (4 << 20))
    return pl.pallas_call(
        functools.partial(_attn_body, chunks=chunks),
        grid=(b, MLA_HEADS, n // tq),
        in_specs=[pl.BlockSpec((None, tq, _HEAD_SLOT), lambda z, h, i: (z, i, h)),
                  pl.BlockSpec((None, nk, QK_NOPE), lambda z, h, i: (z, 0, h)),
                  pl.BlockSpec((None, nk, _LANES), lambda z, h, i: (z, 0, h % 2)),
                  pl.BlockSpec((None, V_HEAD, nk), lambda z, h, i: (z, h, 0))],
        out_specs=pl.BlockSpec((None, tq, V_HEAD), lambda z, h, i: (z, i, h)),
        out_shape=jax.ShapeDtypeStruct((b, n, MLA_HEADS * V_HEAD), _BF16),
        scratch_shapes=[pltpu.VMEM((nk, _HEAD_SLOT), _BF16), pltpu.VMEM((V_HEAD + _SUBLANES_BF16, nk), _BF16)],
        compiler_params=_params(("parallel", "parallel", "arbitrary"), vmem),
        name="mla_attention",
    )(q, k, kr, vt)


def _rope_tables(n):
    rows_n = n // GRID_W
    row = jnp.repeat(jnp.arange(rows_n, dtype=_F32), GRID_W)
    col = jnp.tile(jnp.arange(GRID_W, dtype=_F32), rows_n)
    n_freq = QK_ROPE // 4
    inv_freq = ROPE_BASE ** (-jnp.arange(n_freq, dtype=_F32) / n_freq)
    ar, ac = row[:, None] * inv_freq, col[:, None] * inv_freq
    zeros = jnp.zeros((n, _LANES - QK_ROPE), _F32)
    cos = jnp.concatenate([jnp.cos(ar), jnp.cos(ar), jnp.cos(ac), jnp.cos(ac)], axis=1)
    sin = jnp.concatenate([-jnp.sin(ar), jnp.sin(ar), -jnp.sin(ac), jnp.sin(ac)], axis=1)
    return (jnp.concatenate([cos, zeros, zeros, cos], axis=1), jnp.concatenate([sin, zeros, zeros, sin], axis=1))


def _mla_mix(h, hc, w_a, g_q, g_kv, w_uq, w_ukv, w_o, w_o_sel, x, gate):
    b, n, d = h.shape
    nc = hc.shape[1]
    rope = _rope_tables(n)
    a_width = Q_LORA + KV_LORA + QK_ROPE
    a_pad = -(-a_width // _HEAD_SLOT) * _HEAD_SLOT
    w_a_pad = jnp.pad(w_a, ((0, 0), (0, a_pad - a_width))).astype(_BF16)[None]
    tm = _tile(n, 1024, _SUBLANES_BF16)
    a_l = _matmul(h, w_a_pad, tm=tm, tn=_tile(a_pad, 3 * _HEAD_SLOT, _HEAD_SLOT), out_dtype=_F32,
                  name="mla_down_latent")
    a_c = _matmul(hc, w_a_pad[:, :, Q_LORA:], tm=_tile(nc, 1024, _SUBLANES_BF16), tn=_HEAD_SLOT,
                  out_dtype=_F32, name="mla_down_context")
    cq, ckv_l, kr_l = _mla_prep(a_l, g_q, g_kv, rope)
    ckv_c, kr_c = _mla_prep(a_c, None, g_kv, None)
    ckv = jnp.concatenate([ckv_c, ckv_l], axis=1)
    kr = jnp.concatenate([kr_c, kr_l], axis=1)

    head = QK_NOPE + QK_ROPE
    q, w_o_bf16 = _matmul(cq, w_uq[None], tm=tm, tn=_tile(MLA_HEADS * head, 4 * head, 2 * head), out_dtype=_BF16,
                          rope=rope, q_scale=float(head ** -0.5 * _LOG2E), side_cast=(w_o, w_o_sel),
                          name="mla_q_up")
    nkv = nc + n
    w_ukv_h = w_ukv.reshape(KV_LORA, MLA_HEADS // 2, 2, QK_NOPE + V_HEAD)
    w_uk_h = w_ukv_h[..., :QK_NOPE]
    w_uk_h = jnp.stack([w_uk_h[:, :, 0], jnp.roll(w_uk_h[:, :, 1], -(QK_NOPE // 2), axis=-1)], axis=2)
    w_uk = w_uk_h.reshape(KV_LORA, MLA_HEADS * QK_NOPE).astype(_BF16)[None]
    w_ukv_h = w_ukv_h.reshape(KV_LORA, MLA_HEADS, QK_NOPE + V_HEAD)
    w_uv_t = w_ukv_h[:, :, QK_NOPE:].reshape(KV_LORA, MLA_HEADS * V_HEAD).T.astype(_BF16)[None]
    k = _matmul(ckv, w_uk, tm=_tile(nkv, 1088, _SUBLANES_BF16), tn=_tile(MLA_HEADS * QK_NOPE, 2048, _LANES),
                out_dtype=_BF16, name="mla_k_up")
    vt = _matmul(w_uv_t, ckv, nt=True, tm=_tile(MLA_HEADS * V_HEAD, 1024, _SUBLANES_BF16),
                 tn=_tile(nkv, 2176, _LANES), out_dtype=_BF16, name="mla_v_up_transposed")
    o = _attention(q, k, kr, vt)
    return _matmul(o, w_o_bf16[None], tm=tm, tn=_tile(d, 256, _LANES), out_dtype=_F32, resid=x, gate=gate,
                   name="mla_out_proj")


def _ffn(x, h, w_gate, w_up, w_down, layer, gate, w_down_bf16=None):
    b, n, d = x.shape
    hidden = w_gate.shape[2]
    tm = _tile(n, 1024, _SUBLANES_BF16)
    u = _matmul(h, w_gate, b2=w_up, b_sel=layer, tm=tm, tn=_tile(hidden, 256, _LANES), out_dtype=_BF16,
                side_cast=None if w_down_bf16 is not None else (w_down, layer), name="ffn_gate_up")
    if w_down_bf16 is None:
        u, w_down_bf16 = u
    out = _matmul(u, w_down_bf16[None], tm=_tile(n, 512, _SUBLANES_BF16), tn=_tile(d, 512, _LANES),
                  out_dtype=_F32, resid=x, gate=gate, name="ffn_down")
    return out, w_down_bf16


def kernel(x, c, ctx, c_ctx, w_ada, b_ada, g_mix, g_ffn, fourier_w_out, mla_w_a, mla_g_q, mla_g_kv, mla_w_uq,
           mla_w_ukv, mla_w_o, w_gate, w_up, w_down, g_final):
    b, n, d = x.shape
    depth = w_ada.shape[0]
    cond = jnp.concatenate([c, c_ctx[None]], axis=0)
    cond = jnp.pad(cond, ((0, -(b + 1) % 8), (0, 0)))
    mod = _ada_modulation(cond, w_ada, b_ada)
    dg = d // FOURIER_GROUPS
    tables = _fourier_tables(dg)

    xc = ctx
    for i in range(depth):
        need_ctx = i < depth - 1
        j = i // 2
        lat = [mod[i, :b, s * d:(s + 1) * d].reshape(b, 1, d) for s in range(N_MOD)]
        con = [mod[i, b:b + 1, s * d:(s + 1) * d].reshape(1, 1, d) for s in range(N_MOD)]
        h = _rms_norm(x, g_mix[i], lat[1], lat[0], out_dtype=_BF16)
        hc = _rms_norm(xc, g_mix[i], con[1], con[0], out_dtype=_BF16)
        tm = _tile(n, 1024, _SUBLANES_BF16)
        if i % 2 == 0:
            f, w_out = _fourier_2d(h, tables, side_cast=(fourier_w_out, j))
            w_out = w_out[None]
            x = _matmul(f, w_out, tm=tm, tn=_tile(d, 512, _LANES), out_dtype=_F32, resid=x, gate=lat[2],
                        name="fourier_out_proj")
            if need_ctx:
                xc = _matmul(_fourier_2d(hc, tables), w_out, tm=_tile(xc.shape[1], 1024, _SUBLANES_BF16),
                             tn=_tile(d, 512, _LANES), out_dtype=_F32, resid=xc, gate=con[2],
                             name="fourier_out_proj_ctx")
        else:
            assert not need_ctx, "context queries are only implemented for the Fourier mixer layers"
            x = _mla_mix(h, hc, mla_w_a[j], mla_g_q[j], mla_g_kv[j], mla_w_uq[j], mla_w_ukv[j], mla_w_o, j,
                         x, lat[2])
        x, w_down_i = _ffn(x, _rms_norm(x, g_ffn[i], lat[4], lat[3], out_dtype=_BF16), w_gate, w_up, w_down, i,
                           lat[5])
        if need_ctx:
            hc2 = _rms_norm(xc, g_ffn[i], con[4], con[3], out_dtype=_BF16)
            nc = xc.shape[1]
            xc, _ = _ffn(xc.reshape(1, b * nc, d), hc2.reshape(1, b * nc, d), w_gate, w_up, w_down, i, con[5],
                         w_down_bf16=w_down_i)
            xc = xc.reshape(b, nc, d)
    return _rms_norm(x, g_final, out_dtype=x.dtype)
```
